```python
import math
import jax
import jax.numpy as jnp
from jax import lax
import numpy as np

D_MODEL = 1024
BATCH = 4
SEQ = 4096
DEPTH = 2
DEC_BATCH = 128
DEC_SEQ = 8
PAST_LEN = 8192
PAGE_SIZE = 128

HEAD_DIM = 64
FOX_HEADS = 4
MOBA_HEADS = 4
MLA_HEADS = 4
MLA_Q_RANK = 256
MLA_KV_RANK = 128
MLA_NOPE_DIM = 64
MLA_ROPE_DIM = 32
MLA_V_DIM = 64
CONV_CH = 256
CONV_WIDTH = 31
N_BRANCH = 4
BRANCH_WIDTH = 256
MOBA_BLOCK = 256
MOBA_TOPK = 3
REL_BUCKETS = 32
REL_MAX_DIST = 128
Q_BLOCK = 128
MOBA_Q_BLOCK = 64
D_FF = 2816
N_EXPERTS = 8
TOP_K = 2
D_FF_EXPERT = 3584
ROPE_THETA = 10000.0
NORM_EPS = 1e-6
POOL_SLACK = 4
IN_SIZES = (FOX_HEADS * HEAD_DIM, FOX_HEADS * HEAD_DIM, FOX_HEADS * HEAD_DIM, FOX_HEADS,
            MLA_Q_RANK, MLA_KV_RANK, MLA_ROPE_DIM,
            MOBA_HEADS * HEAD_DIM, MOBA_HEADS * HEAD_DIM, MOBA_HEADS * HEAD_DIM,
            CONV_CH, CONV_CH, N_BRANCH * D_MODEL)
IN_WIDTH = sum(IN_SIZES)

kernel_name = 'hybrid_fox_mla_moba_conformer_step'


def _in_offsets():
    offs, acc = [], 0
    for s in IN_SIZES[:-1]:
        acc += s
        offs.append(acc)
    return offs


def rms_norm(x, g):
    xf = x.astype(jnp.float32)
    y = xf * lax.rsqrt(jnp.mean(xf * xf, axis=-1, keepdims=True) + NORM_EPS)
    return (y * g.astype(jnp.float32)).astype(x.dtype)


def layer_norm(x, g, b):
    xf = x.astype(jnp.float32)
    mu = jnp.mean(xf, axis=-1, keepdims=True)
    var = jnp.mean(jnp.square(xf - mu), axis=-1, keepdims=True)
    return ((xf - mu) * lax.rsqrt(var + NORM_EPS) * g.astype(jnp.float32) + b.astype(jnp.float32)).astype(x.dtype)


def rope(x, pos):
    half = x.shape[-1] // 2
    inv = ROPE_THETA ** (-jnp.arange(half, dtype=jnp.float32) / half)
    ang = pos.astype(jnp.float32)[:, None] * inv[None, :]
    shape = (1, pos.shape[0]) + (1,) * (x.ndim - 3) + (half,)
    cos, sin = jnp.cos(ang).reshape(shape), jnp.sin(ang).reshape(shape)
    xf = x.astype(jnp.float32)
    x1, x2 = xf[..., :half], xf[..., half:]
    return jnp.concatenate([x1 * cos - x2 * sin, x1 * sin + x2 * cos], axis=-1).astype(x.dtype)


def rel_bucket(dist):
    n = jnp.maximum(dist, 0)
    max_exact = REL_BUCKETS // 2
    nf = jnp.maximum(n, 1).astype(jnp.float32)
    large = max_exact + (jnp.log(nf / max_exact) / math.log(REL_MAX_DIST / max_exact)
                         * (REL_BUCKETS - max_exact)).astype(jnp.int32)
    return jnp.where(n < max_exact, n, jnp.minimum(large, REL_BUCKETS - 1))


def map_query_blocks(fn, block, *args):
    n_q = args[0].shape[1]
    if n_q <= block or n_q % block:
        return fn(*args)
    nb = n_q // block
    split = lambda a: jnp.moveaxis(a.reshape((a.shape[0], nb, block) + a.shape[2:]), 1, 0)
    out = lax.map(lambda xs: fn(*xs), tuple(split(a) for a in args))
    return jnp.moveaxis(out, 0, 1).reshape((out.shape[1], n_q) + out.shape[3:])


def fox_attention(q, k, v, cum_q, cum_k, q_pos):
    k_pos = jnp.arange(k.shape[1])
    cum_kT = jnp.transpose(cum_k, (0, 2, 1))
    scale = q.shape[-1] ** -0.5

    def block(qb, cqb, pb):
        s = jnp.einsum('bqhd,bkhd->bhqk', qb, k).astype(jnp.float32) * scale
        s = s + jnp.transpose(cqb, (0, 2, 1))[..., None] - cum_kT[:, :, None, :]
        s = jnp.where(k_pos[None, :] <= pb[0][:, None], s, -jnp.inf)
        p = jax.nn.softmax(s, axis=-1)
        return jnp.einsum('bhqk,bkhd->bqhd', p.astype(v.dtype), v)

    return map_query_blocks(block, Q_BLOCK, q, cum_q, q_pos)


def mla_attention(q_lat, q_rope, ckv, krope, q_pos):
    k_pos = jnp.arange(ckv.shape[1])
    scale = (MLA_NOPE_DIM + MLA_ROPE_DIM) ** -0.5

    def block(ql, qr, pb):
        s = (jnp.einsum('bqhr,bkr->bhqk', ql, ckv) + jnp.einsum('bqhd,bkd->bhqk', qr, krope)).astype(jnp.float32) * scale
        s = jnp.where(k_pos[None, :] <= pb[0][:, None], s, -jnp.inf)
        p = jax.nn.softmax(s, axis=-1)
        return jnp.einsum('bhqk,bkr->bqhr', p.astype(ckv.dtype), ckv)

    return map_query_blocks(block, Q_BLOCK, q_lat, q_rope, q_pos)


def moba_attention(q, k, v, q_pos, rel_bias):
    B, L, H, D = k.shape
    nb = -(-L // MOBA_BLOCK)
    pad = nb * MOBA_BLOCK - L
    if pad:
        k = jnp.pad(k, ((0, 0), (0, pad), (0, 0), (0, 0)))
        v = jnp.pad(v, ((0, 0), (0, pad), (0, 0), (0, 0)))
    kb = k.reshape(B, nb, MOBA_BLOCK, H, D)
    vb = v.reshape(B, nb, MOBA_BLOCK, H, D)
    k_mean = jnp.mean(kb.astype(jnp.float32), axis=2)
    n_top = min(MOBA_TOPK, nb)
    tab = rel_bias.T.astype(jnp.float32)
    b_i = jnp.arange(B)[:, None, None, None]
    h_i = jnp.arange(H)[None, :, None, None]
    h5 = jnp.arange(H)[None, :, None, None, None]
    offs = jnp.arange(MOBA_BLOCK)
    blk_ids = jnp.arange(nb)
    scale = D ** -0.5

    def block(qb, pb):
        pos = pb[0]
        own = pos // MOBA_BLOCK
        gate = jnp.einsum('bqhd,bnhd->bhqn', qb.astype(jnp.float32), k_mean)
        gate = jnp.where(blk_ids[None, :] < own[:, None], gate, -jnp.inf)
        _, top = lax.top_k(gate, n_top)
        slot_ok = jnp.arange(n_top)[None, :] < jnp.minimum(own, n_top)[:, None]
        top = jnp.where(slot_ok, top, own[:, None])
        idx = jnp.concatenate([top, jnp.broadcast_to(own[:, None], top.shape[:-1] + (1,))], axis=-1)
        ok = jnp.concatenate([slot_ok, jnp.ones_like(slot_ok[:, :1])], axis=-1)
        ks = kb[b_i, idx, :, h_i]
        vs = vb[b_i, idx, :, h_i]
        k_pos = idx[..., None] * MOBA_BLOCK + offs
        dist = pos[:, None, None] - k_pos
        s = jnp.einsum('bqhd,bhqjkd->bhqjk', qb, ks).astype(jnp.float32) * scale + tab[h5, rel_bucket(dist)]
        s = jnp.where(ok[:, :, None] & (dist >= 0), s, -jnp.inf)
        p = jax.nn.softmax(s.reshape(s.shape[:3] + (-1,)), axis=-1).reshape(s.shape)
        return jnp.einsum('bhqjk,bhqjkd->bqhd', p.astype(vs.dtype), vs)

    return map_query_blocks(block, MOBA_Q_BLOCK, q, q_pos)


def conv_module(u, g, buf, w_dw, b_dw, ln_g, ln_b):
    a = u * jax.nn.sigmoid(g)
    full = jnp.concatenate([buf.astype(a.dtype), a], axis=1)
    y = lax.conv_general_dilated(full, w_dw[:, None, :].astype(a.dtype), (1,), 'VALID',
                                 dimension_numbers=('NWC', 'WIO', 'NWC'),
                                 feature_group_count=a.shape[-1]) + b_dw.astype(a.dtype)
    y = jax.nn.silu(layer_norm(y, ln_g, ln_b))
    return y, full[:, full.shape[1] - (CONV_WIDTH - 1):]


def token_mixers(h, q_pos, past, conv_buf, l, P):
    B, T, _ = h.shape
    z = jnp.einsum('btd,de->bte', h, P['w_in'][l])
    (fq, fk, fv, ff, mcq, mckv, mkr, bq, bk, bv, cu, cg, gate) = jnp.split(z, _in_offsets(), axis=-1)
    pos = q_pos[0]

    def with_past(new, i):
        return new if past is None else jnp.concatenate([past[i], new], axis=1)

    fox_q = fq.reshape(B, T, FOX_HEADS, HEAD_DIM)
    fox_k = fk.reshape(B, T, FOX_HEADS, HEAD_DIM)
    fox_v = fv.reshape(B, T, FOX_HEADS, HEAD_DIM)
    fox_logf = jax.nn.log_sigmoid(ff.astype(jnp.float32) + P['b_fox_f'][l].astype(jnp.float32))
    cum = jnp.cumsum(with_past(fox_logf, 2).astype(jnp.float32), axis=1)
    o_fox = fox_attention(fox_q, with_past(fox_k, 0), with_past(fox_v, 1), cum[:, -T:], cum, q_pos)

    q = jnp.einsum('btr,rhe->bthe', rms_norm(mcq, P['g_mla_q'][l]), P['w_mla_uq'][l])
    q_nope, q_rope = q[..., :MLA_NOPE_DIM], rope(q[..., MLA_NOPE_DIM:], pos)
    ckv = rms_norm(mckv, P['g_mla_kv'][l])
    krope = rope(mkr, pos)
    w_ukv = P['w_mla_ukv'][l]
    q_lat = jnp.einsum('bthn,rhn->bthr', q_nope, w_ukv[..., :MLA_NOPE_DIM])
    o_lat = mla_attention(q_lat, q_rope, with_past(ckv, 3), with_past(krope, 4), q_pos)
    o_mla = jnp.einsum('bthr,rhv->bthv', o_lat, w_ukv[..., MLA_NOPE_DIM:])

    moba_q = bq.reshape(B, T, MOBA_HEADS, HEAD_DIM)
    moba_k = bk.reshape(B, T, MOBA_HEADS, HEAD_DIM)
    moba_v = bv.reshape(B, T, MOBA_HEADS, HEAD_DIM)
    o_moba = moba_attention(moba_q, with_past(moba_k, 5), with_past(moba_v, 6), q_pos, P['rel_bias'])

    y_conv, new_buf = conv_module(cu, cg, conv_buf, P['w_dw'][l], P['b_dw'][l], P['conv_ln_g'][l], P['conv_ln_b'][l])

    branches = jnp.stack([o_fox.reshape(B, T, BRANCH_WIDTH), o_mla.reshape(B, T, BRANCH_WIDTH),
                          o_moba.reshape(B, T, BRANCH_WIDTH), y_conv.astype(o_fox.dtype)], axis=2)
    proj = jnp.einsum('btnc,ncd->btnd', branches, P['w_branch'][l])
    merged = jnp.sum(jax.nn.sigmoid(gate.reshape(B, T, N_BRANCH, D_MODEL)) * proj, axis=2)
    out = jnp.einsum('btd,de->bte', merged, P['w_out'][l])
    rows = (fox_k, fox_v, fox_logf, ckv, krope, moba_k, moba_v, new_buf)
    return out, rows


def swiglu(h, w_gate, w_up, w_down):
    a = jax.nn.silu(jnp.einsum('btd,df->btf', h, w_gate)) * jnp.einsum('btd,df->btf', h, w_up)
    return jnp.einsum('btf,fd->btd', a, w_down)


def moe_ffn(h, w_router, b_router, w_gate, w_up, w_down):
    logits = jnp.einsum('btd,de->bte', h, w_router).astype(jnp.float32) + b_router.astype(jnp.float32)
    top_val, top_idx = lax.top_k(logits, TOP_K)
    wts = jax.nn.softmax(top_val, axis=-1)
    comb = jnp.sum(jax.nn.one_hot(top_idx, N_EXPERTS, dtype=jnp.float32) * wts[..., None], axis=-2).astype(h.dtype)
    out = jnp.zeros_like(h)
    for e in range(N_EXPERTS):
        out = out + comb[..., e:e + 1] * swiglu(h, w_gate[e], w_up[e], w_down[e])
    return out


def trunk(x, c, q_pos, past_fn, conv_buf_fn, P):
    per_layer = []
    for l in range(DEPTH):
        mod = (jnp.einsum('bd,de->be', jax.nn.silu(c), P['w_ada'][l]) + P['b_ada'][l]).astype(x.dtype)
        sh1, sc1, g1, sh2, sc2, g2 = jnp.split(mod[:, None, :], 6, axis=-1)
        h = rms_norm(x, P['norm_mix'][l]) * (1 + sc1) + sh1
        mix, rows = token_mixers(h, q_pos, past_fn(l), conv_buf_fn(l), l, P)
        x = x + g1 * mix
        h = rms_norm(x, P['norm_ffn'][l]) * (1 + sc2) + sh2
        i = l // 2
        if l % 2 == 0:
            f = swiglu(h, P['ffn_w_gate'][i], P['ffn_w_up'][i], P['ffn_w_down'][i])
        else:
            f = moe_ffn(h, P['moe_w_router'][i], P['moe_b_router'][i], P['moe_w_gate'][i],
                        P['moe_w_up'][i], P['moe_w_down'][i])
        x = x + g2 * f
        per_layer.append(rows)
    y = rms_norm(x, P['norm_out'])
    stacked = [jnp.stack([r[j] for r in per_layer]) for j in range(len(per_layer[0]))]
    return y, stacked


def setup_inputs(seed: int = 0) -> dict:
    key = jax.random.key(seed)
    ks = iter(jax.random.split(key, 64))
    f32 = jnp.float32

    def nrm(shape, scale=1.0):
        return jax.random.normal(next(ks), shape, f32) * scale

    def gain(shape):
        return 1.0 + 0.05 * jax.random.normal(next(ks), shape, f32)

    n_pages = PAST_LEN // PAGE_SIZE
    n_used = DEC_BATCH * n_pages
    n_pool = n_used + n_used // POOL_SLACK
    page_table = jax.random.permutation(next(ks), n_pool)[:n_used].reshape(DEC_BATCH, n_pages).astype(jnp.int32)
    pool = (DEPTH, n_pool, PAGE_SIZE)
    n_dense = (DEPTH + 1) // 2
    n_moe = DEPTH // 2
    D = D_MODEL
    return {
        'x_prompt': nrm((BATCH, SEQ, D)),
        'x_sample': nrm((DEC_BATCH, DEC_SEQ, D)),
        'cache_fox_k': nrm(pool + (FOX_HEADS, HEAD_DIM)),
        'cache_fox_v': nrm(pool + (FOX_HEADS, HEAD_DIM)),
        'cache_fox_logf': jax.nn.log_sigmoid(nrm(pool + (FOX_HEADS,)) + 3.0),
        'cache_mla_ckv': nrm(pool + (MLA_KV_RANK,)),
        'cache_mla_krope': nrm(pool + (MLA_ROPE_DIM,)),
        'cache_moba_k': nrm(pool + (MOBA_HEADS, HEAD_DIM)),
        'cache_moba_v': nrm(pool + (MOBA_HEADS, HEAD_DIM)),
        'state_conv': nrm((DEPTH, DEC_BATCH, CONV_WIDTH - 1, CONV_CH), 0.5),
        'page_table': page_table,
        'c_prompt': nrm((BATCH, D)),
        'c_sample': nrm((DEC_BATCH, D)),
        'w_ada': nrm((DEPTH, D, 6 * D), 0.5 * D ** -0.5),
        'b_ada': nrm((DEPTH, 6 * D), 0.02),
        'norm_mix': gain((DEPTH, D)),
        'norm_ffn': gain((DEPTH, D)),
        'w_in': nrm((DEPTH, D, IN_WIDTH), D ** -0.5),
        'b_fox_f': jax.random.uniform(next(ks), (DEPTH, FOX_HEADS), f32, 1.0, 6.0),
        'g_mla_q': gain((DEPTH, MLA_Q_RANK)),
        'g_mla_kv': gain((DEPTH, MLA_KV_RANK)),
        'w_mla_uq': nrm((DEPTH, MLA_Q_RANK, MLA_HEADS, MLA_NOPE_DIM + MLA_ROPE_DIM), MLA_Q_RANK ** -0.5),
        'w_mla_ukv': nrm((DEPTH, MLA_KV_RANK, MLA_HEADS, MLA_NOPE_DIM + MLA_V_DIM), MLA_KV_RANK ** -0.5),
        'w_dw': nrm((DEPTH, CONV_WIDTH, CONV_CH), CONV_WIDTH ** -0.5),
        'b_dw': nrm((DEPTH, CONV_CH), 0.02),
        'conv_ln_g': gain((DEPTH, CONV_CH)),
        'conv_ln_b': nrm((DEPTH, CONV_CH), 0.02),
        'w_branch': nrm((DEPTH, N_BRANCH, BRANCH_WIDTH, D), BRANCH_WIDTH ** -0.5),
        'w_out': nrm((DEPTH, D, D), D ** -0.5),
        'rel_bias': nrm((REL_BUCKETS, MOBA_HEADS), 0.5),
        'ffn_w_gate': nrm((n_dense, D, D_FF), D ** -0.5),
        'ffn_w_up': nrm((n_dense, D, D_FF), D ** -0.5),
        'ffn_w_down': nrm((n_dense, D_FF, D), D_FF ** -0.5),
        'moe_w_router': nrm((n_moe, D, N_EXPERTS), D ** -0.5),
        'moe_b_router': nrm((n_moe, N_EXPERTS), 0.01),
        'moe_w_gate': nrm((n_moe, N_EXPERTS, D, D_FF_EXPERT), D ** -0.5),
        'moe_w_up': nrm((n_moe, N_EXPERTS, D, D_FF_EXPERT), D ** -0.5),
        'moe_w_down': nrm((n_moe, N_EXPERTS, D_FF_EXPERT, D), D_FF_EXPERT ** -0.5),
        'norm_out': gain((D,)),
    }


def reference(x_prompt, x_sample, cache_fox_k, cache_fox_v, cache_fox_logf, cache_mla_ckv, cache_mla_krope,
              cache_moba_k, cache_moba_v, state_conv, page_table, c_prompt, c_sample,
              w_ada, b_ada, norm_mix, norm_ffn, w_in, b_fox_f, g_mla_q, g_mla_kv, w_mla_uq, w_mla_ukv,
              w_dw, b_dw, conv_ln_g, conv_ln_b, w_branch, w_out, rel_bias,
              ffn_w_gate, ffn_w_up, ffn_w_down, moe_w_router, moe_b_router, moe_w_gate, moe_w_up, moe_w_down,
              norm_out):
    P = dict(w_ada=w_ada, b_ada=b_ada, norm_mix=norm_mix, norm_ffn=norm_ffn, w_in=w_in, b_fox_f=b_fox_f,
             g_mla_q=g_mla_q, g_mla_kv=g_mla_kv, w_mla_uq=w_mla_uq, w_mla_ukv=w_mla_ukv, w_dw=w_dw, b_dw=b_dw,
             conv_ln_g=conv_ln_g, conv_ln_b=conv_ln_b, w_branch=w_branch, w_out=w_out, rel_bias=rel_bias,
             ffn_w_gate=ffn_w_gate, ffn_w_up=ffn_w_up, ffn_w_down=ffn_w_down, moe_w_router=moe_w_router,
             moe_b_router=moe_b_router, moe_w_gate=moe_w_gate, moe_w_up=moe_w_up, moe_w_down=moe_w_down,
             norm_out=norm_out)

    seq = x_prompt.shape[1]
    pos_prompt = jnp.arange(seq, dtype=jnp.int32)[None, :]
    zero_buf = jnp.zeros((x_prompt.shape[0], CONV_WIDTH - 1, CONV_CH), x_prompt.dtype)
    y_prompt, sp = trunk(x_prompt, c_prompt, pos_prompt, lambda l: None, lambda l: zero_buf, P)

    db = page_table.shape[0]
    past_len = page_table.shape[1] * cache_fox_k.shape[2]
    pos_sample = (past_len + jnp.arange(x_sample.shape[1], dtype=jnp.int32))[None, :]

    def gather(pool, l):
        g = pool[l, page_table]
        return g.reshape((db, past_len) + g.shape[3:])

    def past_fn(l):
        return (gather(cache_fox_k, l), gather(cache_fox_v, l), gather(cache_fox_logf, l),
                gather(cache_mla_ckv, l), gather(cache_mla_krope, l),
                gather(cache_moba_k, l), gather(cache_moba_v, l))

    y_sample, ss = trunk(x_sample, c_sample, pos_sample, past_fn, lambda l: state_conv[l], P)

    p_fox_k, p_fox_v, p_fox_logf, p_mla_ckv, p_mla_krope, p_moba_k, p_moba_v, p_conv = sp
    s_fox_k, s_fox_v, s_fox_logf, s_mla_ckv, s_mla_krope, s_moba_k, s_moba_v, s_conv = ss
    return (y_prompt, y_sample, p_fox_k, p_fox_v, p_fox_logf, p_mla_ckv, p_mla_krope, p_moba_k, p_moba_v, p_conv,
            s_fox_k, s_fox_v, s_fox_logf, s_mla_ckv, s_mla_krope, s_moba_k, s_moba_v, s_conv)
```

```python
import functools
import math

import numpy as np
import jax
import jax.numpy as jnp
from jax import lax
from jax.experimental import pallas as pl
from jax.experimental.pallas import tpu as pltpu

F32 = jnp.float32
BF = jnp.bfloat16

D_MODEL = 1024
N_HEADS = 4
HEAD_DIM = 64
BRANCH_W = 256
MLA_Q_RANK = 256
MLA_KV_RANK = 128
MLA_NOPE = 64
MLA_ROPE = 32
CONV_W = 31
MOBA_BLOCK = 256
MOBA_TOPK = 3
REL_BUCKETS = 32
REL_MAX_DIST = 128
ROPE_THETA = 10000.0
EPS = 1e-6
PAGE = 128
IN_SIZES = (256, 256, 256, 4, 256, 128, 32, 256, 256, 256, 256, 256, 4096)

LANES = 128
CHUNK = 512
N_CHUNKS = 13
CH_FQK, CH_FV_BQ, CH_BKV, CH_MLA, CH_CONV = 8, 9, 10, 11, 12
NEG = -1e30
VMEM_LIMIT = 56 * 2 ** 20


def _params(sem):
    return pltpu.CompilerParams(dimension_semantics=sem, vmem_limit_bytes=VMEM_LIMIT)


def _split3(x):
    x1 = x.astype(BF)
    r1 = x - x1.astype(F32)
    x2 = r1.astype(BF)
    x3 = (r1 - x2.astype(F32)).astype(BF)
    return x1, x2, x3


def _dot_exact01(x, m01):
    acc = None
    for piece in _split3(x):
        t = jnp.dot(piece, m01, preferred_element_type=F32)
        acc = t if acc is None else acc + t
    return acc


def _dot_f32_nt(a, b):
    a1, a2, a3 = _split3(a)
    b1, b2, b3 = _split3(b)
    dn = (((1,), (1,)), ((), ()))
    acc = None
    for x, y in ((a1, b1), (a1, b2), (a2, b1), (a2, b2), (a1, b3), (a3, b1)):
        t = lax.dot_general(x, y, dn, preferred_element_type=F32)
        acc = t if acc is None else acc + t
    return acc


def _dot_nt(a, b):
    return lax.dot_general(a, b, (((1,), (1,)), ((), ())), preferred_element_type=F32)


def _log_sigmoid(x):
    return jnp.minimum(x, 0.0) - jnp.log1p(jnp.exp(-jnp.abs(x)))


def _rel_thresholds():
    max_exact = REL_BUCKETS // 2
    n = np.arange(0, 4 * REL_MAX_DIST)
    nf = np.maximum(n, 1).astype(np.float64)
    large = max_exact + (np.log(nf / max_exact) / math.log(REL_MAX_DIST / max_exact)
                         * (REL_BUCKETS - max_exact)).astype(np.int64)
    bucket = np.where(n < max_exact, n, np.minimum(large, REL_BUCKETS - 1))
    return [int(np.argmax(bucket >= k)) for k in range(1, REL_BUCKETS)]


_REL_THR = _rel_thresholds()


class _Tok:
    def __init__(self, B, T, bb, tt):
        assert B % bb == 0 and T % tt == 0 and (bb == 1 or tt == T) and tt % 8 == 0
        self.B, self.T, self.bb, self.tt = B, T, bb, tt
        self.nt = T // tt
        self.n_tiles = (B // bb) * self.nt
        self.tm = bb * tt
        self.N = B * T

    def x_spec(self):
        nt = self.nt
        return pl.BlockSpec((self.bb, self.tt, D_MODEL), lambda i, *_: (i // nt, i % nt, 0))

    def mod_spec(self, l, k):
        nt = self.nt
        return pl.BlockSpec((None, None, self.bb, 1, D_MODEL), lambda i, *_: (l, k, i // nt, 0, 0))

    def flat_spec(self, width):
        return pl.BlockSpec((self.tm, width), lambda i, *_: (i, 0))

    def z_spec(self, chunk, n=None):
        if n is None:
            return pl.BlockSpec((None, self.tm, CHUNK), lambda i, *_: (chunk, i, 0))
        return pl.BlockSpec((n, self.tm, CHUNK), lambda i, *_: (chunk // n, i, 0))


def _ada_kernel(c_ref, w_ref, b_ref, o_ref):
    c = c_ref[...]
    s = (c * jax.nn.sigmoid(c)).astype(BF)
    o_ref[0] = jnp.dot(s, w_ref[0].astype(BF), preferred_element_type=F32) + b_ref[0]


def _ada(c_all, w_ada, b_ada):
    L, D, D6 = w_ada.shape
    Mc = c_all.shape[0]
    tn = CHUNK
    return pl.pallas_call(
        _ada_kernel, grid=(L, D6 // tn),
        in_specs=[pl.BlockSpec((Mc, D), lambda l, j: (0, 0)),
                  pl.BlockSpec((1, D, tn), lambda l, j: (l, 0, j)),
                  pl.BlockSpec((1, 1, tn), lambda l, j: (l, 0, j))],
        out_specs=pl.BlockSpec((1, Mc, tn), lambda l, j: (l, 0, j)),
        out_shape=jax.ShapeDtypeStruct((L, Mc, D6), F32),
        compiler_params=_params(("arbitrary", "arbitrary")), name="ada",
    )(c_all, w_ada, b_ada.reshape(L, 1, D6))


def _modulated_norm(x, g, sc, sh):
    ms = jnp.mean(x * x, axis=-1, keepdims=True)
    y = x * lax.rsqrt(ms + EPS) * g
    return y * (1.0 + sc) + sh


def _inproj_kernel(x_ref, sh_ref, sc_ref, g_ref, w_ref, z_ref, h_scr):
    j = pl.program_id(1)

    @pl.when(j == 0)
    def _():
        h = _modulated_norm(x_ref[...], g_ref[...], sc_ref[...], sh_ref[...])
        h_scr[...] = h.reshape(h_scr.shape).astype(BF)

    acc = jnp.dot(h_scr[...], w_ref[...], preferred_element_type=F32)

    @pl.when(j < 8)
    def _():
        z_ref[...] = jax.nn.sigmoid(acc)

    @pl.when(j >= 8)
    def _():
        z_ref[...] = acc


def _inproj(tok, x, mod, l, g_norm, w_perm):
    return pl.pallas_call(
        _inproj_kernel, grid=(tok.n_tiles, N_CHUNKS),
        in_specs=[tok.x_spec(), tok.mod_spec(l, 0), tok.mod_spec(l, 1),
                  pl.BlockSpec((1, D_MODEL), lambda i, j: (0, 0)),
                  pl.BlockSpec((D_MODEL, CHUNK), lambda i, j: (0, j))],
        out_specs=pl.BlockSpec((None, tok.tm, CHUNK), lambda i, j: (j, i, 0)),
        out_shape=jax.ShapeDtypeStruct((N_CHUNKS, tok.N, CHUNK), F32),
        scratch_shapes=[pltpu.VMEM((tok.tm, D_MODEL), BF)],
        compiler_params=_params(("arbitrary", "arbitrary")), name="inproj",
    )(x, mod, mod, g_norm.reshape(1, D_MODEL), w_perm)


def _perm_w_in(w):
    offs = np.concatenate([[0], np.cumsum(IN_SIZES)])
    fq, fk, fv, ff, mcq, mckv, mkr, bq, bk, bv, cu, cg, gate = [w[:, offs[i]:offs[i + 1]] for i in range(13)]
    pad = jnp.zeros((w.shape[0], LANES - MLA_ROPE - N_HEADS), w.dtype)
    return jnp.concatenate([gate, fq, fk, fv, bq, bk, bv, mcq, mckv, mkr, ff, pad, cu, cg], axis=1).astype(BF)


def _rope_lanes(x, c, sa, sb):
    return x * c + pltpu.roll(x, LANES - MLA_ROPE // 2, 1) * sa + pltpu.roll(x, MLA_ROPE // 2, 1) * sb


def _prep_kernel(z_ref, gq_ref, gkv_ref, wn_ref, wr_ref, wuk_ref, bf_ref, c_ref, sa_ref, sb_ref,
                 qlat_ref, qrope_ref, ckv_ref, krlf_ref):
    z = z_ref[...]
    mcq, mckv, kf = z[:, :256], z[:, 256:384], z[:, 384:512]
    qn = (mcq * lax.rsqrt(jnp.mean(mcq * mcq, axis=-1, keepdims=True) + EPS) * gq_ref[...]).astype(BF)
    q_nope = jnp.dot(qn, wn_ref[...], preferred_element_type=F32)
    q_rope = jnp.dot(qn, wr_ref[...], preferred_element_type=F32)
    c, sa, sb = c_ref[...], sa_ref[...], sb_ref[...]
    qrope_ref[...] = _rope_lanes(q_rope, c, sa, sb)
    for h in range(N_HEADS):
        qh = q_nope[:, h * MLA_NOPE:(h + 1) * MLA_NOPE].astype(BF)
        qlat_ref[:, h * MLA_KV_RANK:(h + 1) * MLA_KV_RANK] = jnp.dot(qh, wuk_ref[h], preferred_element_type=F32)
    ckv_ref[...] = mckv * lax.rsqrt(jnp.mean(mckv * mckv, axis=-1, keepdims=True) + EPS) * gkv_ref[...]
    lane = lax.broadcasted_iota(jnp.int32, kf.shape, 1)
    roped = _rope_lanes(kf, c, sa, sb)
    logf = _log_sigmoid(kf + bf_ref[...])
    krlf_ref[...] = jnp.where(lane < MLA_ROPE, roped, jnp.where(lane < MLA_ROPE + N_HEADS, logf, 0.0))


def _prep(tok, z, gq, gkv, wn, wr, wuk, bf128, rope_tabs):
    N = tok.N
    full = lambda a: pl.BlockSpec(a.shape, lambda i: (0,) * a.ndim)
    tab_spec = pl.BlockSpec((tok.tm, LANES), lambda i: (i % max(tok.nt, 1), 0)) if tok.bb == 1 else \
        pl.BlockSpec((tok.tm, LANES), lambda i: (0, 0))
    c, sa, sb = rope_tabs
    return pl.pallas_call(
        _prep_kernel, grid=(tok.n_tiles,),
        in_specs=[tok.z_spec(CH_MLA), full(gq), full(gkv), full(wn), full(wr), full(wuk), full(bf128),
                  tab_spec, tab_spec, tab_spec],
        out_specs=[tok.flat_spec(512), tok.flat_spec(128), tok.flat_spec(128), tok.flat_spec(128)],
        out_shape=[jax.ShapeDtypeStruct((N, 512), F32), jax.ShapeDtypeStruct((N, 128), F32),
                   jax.ShapeDtypeStruct((N, 128), F32), jax.ShapeDtypeStruct((N, 128), F32)],
        compiler_params=_params(("arbitrary",)), name="mla_prep",
    )(z, gq, gkv, wn, wr, wuk, bf128, c, sa, sb)


def _rope_tables(pos, reps):
    half = MLA_ROPE // 2
    inv = ROPE_THETA ** (-jnp.arange(half, dtype=F32) / half)
    ang = pos.astype(F32)[:, None] * inv[None, :]
    cos, sin = jnp.cos(ang), jnp.sin(ang)
    zero = jnp.zeros_like(sin)
    tile = lambda a, b: jnp.tile(jnp.concatenate([a, b], axis=1), (reps, LANES // MLA_ROPE))
    return tile(cos, cos), tile(-sin, zero), tile(zero, sin)


def _cumsum_kernel(x_ref, o_ref):
    x = x_ref[0]
    R = x.shape[0]
    r = lax.broadcasted_iota(jnp.int32, (LANES, LANES), 0)
    c = lax.broadcasted_iota(jnp.int32, (LANES, LANES), 1)
    within = _dot_exact01(x, (r <= c).astype(BF))
    tot = jnp.broadcast_to(within[:, LANES - 1:LANES], (R, LANES))
    rr = lax.broadcasted_iota(jnp.int32, (R, R), 0)
    cc = lax.broadcasted_iota(jnp.int32, (R, R), 1)
    lower = (cc < rr).astype(BF)
    acc = None
    for piece in _split3(tot):
        t = jnp.dot(lower, piece, preferred_element_type=F32)
        acc = t if acc is None else acc + t
    o_ref[0] = within + acc


def _cumsum_time(x):
    G, R, _ = x.shape
    return pl.pallas_call(
        _cumsum_kernel, grid=(G,),
        in_specs=[pl.BlockSpec((1, R, LANES), lambda g: (g, 0, 0))],
        out_specs=pl.BlockSpec((1, R, LANES), lambda g: (g, 0, 0)),
        out_shape=jax.ShapeDtypeStruct(x.shape, F32),
        compiler_params=_params(("arbitrary",)), name="fox_cumsum",
    )(x)


def _online(s, v_bf, m_ref, l_ref, acc_ref, mask=None):
    m_prev = m_ref[...][:, :1]
    m_new = jnp.maximum(m_prev, jnp.max(s, axis=-1, keepdims=True))
    alpha = jnp.exp(m_prev - m_new)
    p = jnp.exp(s - m_new)
    if mask is not None:
        p = jnp.where(mask, p, 0.0)
    l_new = alpha * l_ref[...][:, :1] + jnp.sum(p, axis=-1, keepdims=True)
    l_ref[...] = jnp.broadcast_to(l_new, l_ref.shape)
    m_ref[...] = jnp.broadcast_to(m_new, m_ref.shape)
    acc_ref[...] = alpha * acc_ref[...] + jnp.dot(p.astype(BF), v_bf, preferred_element_type=F32)


def _init_stats(m_scr, l_scr, acc_scr):
    m_scr[...] = jnp.full(m_scr.shape, NEG, F32)
    l_scr[...] = jnp.zeros(l_scr.shape, F32)
    acc_scr[...] = jnp.zeros(acc_scr.shape, F32)


def _fox_kernel(q_ref, k_ref, v_ref, cq_ref, ck_ref, o_ref, m_scr, l_scr, acc_scr, *, tq):
    qi, ki = pl.program_id(1), pl.program_id(2)

    @pl.when(ki == 0)
    def _():
        _init_stats(m_scr, l_scr, acc_scr)

    @pl.when(ki <= qi)
    def _():
        q = q_ref[:, :BRANCH_W]
        k = k_ref[:, BRANCH_W:]
        v = v_ref[:, :BRANCH_W]
        cq = cq_ref[0]
        ck = ck_ref[0]
        row = qi * tq + lax.broadcasted_iota(jnp.int32, (tq, tq), 0)
        col = ki * tq + lax.broadcasted_iota(jnp.int32, (tq, tq), 1)
        causal = col <= row
        for h in range(N_HEADS):
            sl = slice(h * HEAD_DIM, (h + 1) * HEAD_DIM)
            s = _dot_nt(q[:, sl].astype(BF), k[:, sl].astype(BF)) * (HEAD_DIM ** -0.5)
            s = s + cq[:, h:h + 1] - ck[h:h + 1, :]
            s = jnp.where(causal, s, NEG)
            _online(s, v[:, sl].astype(BF), m_scr.at[h], l_scr.at[h], acc_scr.at[:, sl])

    @pl.when(ki == pl.num_programs(2) - 1)
    def _():
        for h in range(N_HEADS):
            sl = slice(h * HEAD_DIM, (h + 1) * HEAD_DIM)
            o_ref[:, sl] = acc_scr[:, sl] / l_scr[h][:, :1]


def _attn_scratch(tq, dv):
    return [pltpu.VMEM((N_HEADS, tq, LANES), F32), pltpu.VMEM((N_HEADS, tq, LANES), F32),
            pltpu.VMEM((tq, dv), F32)]


def _fox_prompt(z, cum, cum_t, B, T, tq):
    nq = T // tq
    zq = lambda ch: pl.BlockSpec((None, tq, CHUNK), lambda b, qi, ki: (ch, b * nq + qi, 0))
    zk = lambda ch: pl.BlockSpec((None, tq, CHUNK), lambda b, qi, ki: (ch, b * nq + jnp.minimum(ki, qi), 0))
    return pl.pallas_call(
        functools.partial(_fox_kernel, tq=tq), grid=(B, nq, nq),
        in_specs=[zq(CH_FQK), zk(CH_FQK), zk(CH_FV_BQ),
                  pl.BlockSpec((1, tq, N_HEADS), lambda b, qi, ki: (b, qi, 0)),
                  pl.BlockSpec((1, N_HEADS, tq), lambda b, qi, ki: (b, 0, jnp.minimum(ki, qi)))],
        out_specs=pl.BlockSpec((tq, BRANCH_W), lambda b, qi, ki: (b * nq + qi, 0)),
        out_shape=jax.ShapeDtypeStruct((B * T, BRANCH_W), F32),
        scratch_shapes=_attn_scratch(tq, BRANCH_W),
        compiler_params=_params(("arbitrary", "arbitrary", "arbitrary")), name="fox_prompt",
    )(z, z, z, cum, cum_t)


_MLA_SCALE = (MLA_NOPE + MLA_ROPE) ** -0.5


def _mla_kernel(ql_ref, qr_ref, ckv_ref, kr_ref, wuv_ref, o_ref, m_scr, l_scr, acc_scr, *, tq):
    qi, ki = pl.program_id(1), pl.program_id(2)

    @pl.when(ki == 0)
    def _():
        _init_stats(m_scr, l_scr, acc_scr)

    @pl.when(ki <= qi)
    def _():
        ckv = ckv_ref[...].astype(BF)
        kr = kr_ref[:, :MLA_ROPE].astype(BF)
        ql = ql_ref[...]
        qr = qr_ref[...]
        row = qi * tq + lax.broadcasted_iota(jnp.int32, (tq, tq), 0)
        col = ki * tq + lax.broadcasted_iota(jnp.int32, (tq, tq), 1)
        causal = col <= row
        for h in range(N_HEADS):
            sl = slice(h * MLA_KV_RANK, (h + 1) * MLA_KV_RANK)
            s = _dot_nt(ql[:, sl].astype(BF), ckv) + \
                _dot_nt(qr[:, h * MLA_ROPE:(h + 1) * MLA_ROPE].astype(BF), kr)
            s = jnp.where(causal, s * _MLA_SCALE, NEG)
            _online(s, ckv, m_scr.at[h], l_scr.at[h], acc_scr.at[:, sl])

    @pl.when(ki == pl.num_programs(2) - 1)
    def _():
        for h in range(N_HEADS):
            sl = slice(h * MLA_KV_RANK, (h + 1) * MLA_KV_RANK)
            o_lat = (acc_scr[:, sl] / l_scr[h][:, :1]).astype(BF)
            o_ref[:, h * HEAD_DIM:(h + 1) * HEAD_DIM] = jnp.dot(o_lat, wuv_ref[h], preferred_element_type=F32)


def _mla_prompt(qlat, qrope, ckv, krlf, wuv, B, T, tq):
    nq = T // tq
    qs = lambda w: pl.BlockSpec((tq, w), lambda b, qi, ki: (b * nq + qi, 0))
    ks = lambda w: pl.BlockSpec((tq, w), lambda b, qi, ki: (b * nq + jnp.minimum(ki, qi), 0))
    return pl.pallas_call(
        functools.partial(_mla_kernel, tq=tq), grid=(B, nq, nq),
        in_specs=[qs(512), qs(128), ks(128), ks(128),
                  pl.BlockSpec(wuv.shape, lambda b, qi, ki: (0, 0, 0))],
        out_specs=pl.BlockSpec((tq, BRANCH_W), lambda b, qi, ki: (b * nq + qi, 0)),
        out_shape=jax.ShapeDtypeStruct((B * T, BRANCH_W), F32),
        scratch_shapes=_attn_scratch(tq, N_HEADS * MLA_KV_RANK),
        compiler_params=_params(("arbitrary", "arbitrary", "arbitrary")), name="mla_prompt",
    )(qlat, qrope, ckv, krlf, wuv)


def _relbias_kernel(tab_ref, qpos_ref, kpos_ref, o_ref):
    dist = jnp.maximum(qpos_ref[...] - kpos_ref[...], 0)
    tab = tab_ref[...]
    out = jnp.broadcast_to(tab[:, 0:1], dist.shape)
    for k in range(1, REL_BUCKETS):
        out = jnp.where(dist >= _REL_THR[k - 1], tab[:, k:k + 1], out)
    o_ref[...] = out


def _relbias(tab_rows, qpos, kpos):
    R, C = tab_rows.shape[0], kpos.shape[1]
    rt = min(R, 512)
    return pl.pallas_call(
        _relbias_kernel, grid=(R // rt,),
        in_specs=[pl.BlockSpec((rt, REL_BUCKETS), lambda i: (i, 0)),
                  pl.BlockSpec((rt, 1), lambda i: (i, 0)),
                  pl.BlockSpec((1, C), lambda i: (0, 0))],
        out_specs=pl.BlockSpec((rt, C), lambda i: (i, 0)),
        out_shape=jax.ShapeDtypeStruct((R, C), F32),
        compiler_params=_params(("arbitrary",)), name="rel_bias",
    )(tab_rows, qpos, kpos)


def _kmean_kernel(z_ref, o_ref):
    o_ref[0] = jnp.mean(z_ref[:, :BRANCH_W], axis=0, keepdims=True)


def _kmean_prompt(z, N):
    nb = N // MOBA_BLOCK
    return pl.pallas_call(
        _kmean_kernel, grid=(nb,),
        in_specs=[pl.BlockSpec((None, MOBA_BLOCK, CHUNK), lambda i: (CH_BKV, i, 0))],
        out_specs=pl.BlockSpec((1, 1, BRANCH_W), lambda i: (i, 0, 0)),
        out_shape=jax.ShapeDtypeStruct((nb, 1, BRANCH_W), F32),
        compiler_params=_params(("arbitrary",)), name="moba_kmean",
    )(z)


def _top_blocks(gate, n_valid, n_blocks):
    lane = lax.broadcasted_iota(jnp.int32, gate.shape, 1)
    rank = jnp.zeros(gate.shape, F32)
    for m in range(n_blocks):
        gm = gate[:, m:m + 1]
        ahead = jnp.where(gm > gate, 1.0, jnp.where(gm == gate, jnp.where(lane > m, 1.0, 0.0), 0.0))
        rank = rank + ahead * jnp.where(m < n_valid, 1.0, 0.0)
    return jnp.where(lane < n_valid, jnp.where(rank < MOBA_TOPK, 1.0, 0.0), 0.0)


def _moba_kernel(q_ref, kv_ref, km_ref, bias_ref, o_ref, m_scr, l_scr, acc_scr, sel_scr, *, nb):
    qi, ki = pl.program_id(1), pl.program_id(2)
    tq = MOBA_BLOCK

    @pl.when(ki == 0)
    def _():
        _init_stats(m_scr, l_scr, acc_scr)
        q = q_ref[:, BRANCH_W:]
        km = km_ref[0]
        for h in range(N_HEADS):
            sl = slice(h * HEAD_DIM, (h + 1) * HEAD_DIM)
            g = _dot_f32_nt(q[:, sl], km[:, sl])
            if nb < LANES:
                g = jnp.concatenate([g, jnp.zeros((tq, LANES - nb), F32)], axis=1)
            sel_scr[h] = _top_blocks(g, qi, nb)

    def process(mask_of_head):
        q = q_ref[:, BRANCH_W:]
        k = kv_ref[:, :BRANCH_W]
        v = kv_ref[:, BRANCH_W:]
        for h in range(N_HEADS):
            sl = slice(h * HEAD_DIM, (h + 1) * HEAD_DIM)
            s = _dot_nt(q[:, sl].astype(BF), k[:, sl].astype(BF)) * (HEAD_DIM ** -0.5) + bias_ref[0, h]
            mask = mask_of_head(h)
            s = jnp.where(mask, s, NEG)
            _online(s, v[:, sl].astype(BF), m_scr.at[h], l_scr.at[h], acc_scr.at[:, sl], mask=mask)

    @pl.when(ki < qi)
    def _():
        lane = lax.broadcasted_iota(jnp.int32, (tq, LANES), 1)
        process(lambda h: jnp.sum(jnp.where(lane == ki, sel_scr[h], 0.0), axis=-1, keepdims=True) > 0.0)

    @pl.when(ki == qi)
    def _():
        row = lax.broadcasted_iota(jnp.int32, (tq, tq), 0)
        col = lax.broadcasted_iota(jnp.int32, (tq, tq), 1)
        process(lambda h: col <= row)

    @pl.when(ki == pl.num_programs(2) - 1)
    def _():
        for h in range(N_HEADS):
            sl = slice(h * HEAD_DIM, (h + 1) * HEAD_DIM)
            o_ref[:, sl] = acc_scr[:, sl] / l_scr[h][:, :1]


def _moba_prompt(z, kmean, bias, B, T):
    tq = MOBA_BLOCK
    nq = T // tq
    assert nq <= LANES
    return pl.pallas_call(
        functools.partial(_moba_kernel, nb=nq), grid=(B, nq, nq),
        in_specs=[pl.BlockSpec((None, tq, CHUNK), lambda b, qi, ki: (CH_FV_BQ, b * nq + qi, 0)),
                  pl.BlockSpec((None, tq, CHUNK), lambda b, qi, ki: (CH_BKV, b * nq + jnp.minimum(ki, qi), 0)),
                  pl.BlockSpec((1, nq, BRANCH_W), lambda b, qi, ki: (b, 0, 0)),
                  pl.BlockSpec((1, N_HEADS, tq, tq),
                               lambda b, qi, ki: (jnp.clip(qi - ki, 0, 2), 0, 0, 0))],
        out_specs=pl.BlockSpec((tq, BRANCH_W), lambda b, qi, ki: (b * nq + qi, 0)),
        out_shape=jax.ShapeDtypeStruct((B * T, BRANCH_W), F32),
        scratch_shapes=_attn_scratch(tq, BRANCH_W) + [pltpu.VMEM((N_HEADS, tq, LANES), F32)],
        compiler_params=_params(("arbitrary", "arbitrary", "arbitrary")), name="moba_prompt",
    )(z, z, kmean, bias)


def _head_rows(x, width):
    return jnp.concatenate([x[:, h * width:(h + 1) * width] for h in range(N_HEADS)], axis=0)


def _blockdiag_rows(q):
    lane_head = lax.broadcasted_iota(jnp.int32, q.shape, 1) // HEAD_DIM
    return jnp.concatenate([jnp.where(lane_head == h, q, 0.0) for h in range(N_HEADS)], axis=0)


def _diag_heads(o_all, tq):
    lane_head = lax.broadcasted_iota(jnp.int32, (tq, o_all.shape[1]), 1) // HEAD_DIM
    out = jnp.zeros((tq, o_all.shape[1]), F32)
    for h in range(N_HEADS):
        out = out + jnp.where(lane_head == h, o_all[h * tq:(h + 1) * tq], 0.0)
    return out


def _rows_per_head(x, tq):
    return jnp.concatenate([jnp.broadcast_to(x[h:h + 1], (tq, x.shape[1])) for h in range(N_HEADS)], axis=0)


def _page_specs(n, width, l, pg, np_):
    return [pl.BlockSpec((None, None, PAGE, width),
                         lambda b, s, pt, p=p: (l, pt[b * np_ + s * pg + p], 0, 0)) for p in range(n)]


def _fox_dec_kernel(pt_ref, zq_ref, zv_ref, lfn_ref, *rest, pg, tq):
    k_refs, v_refs, lf_refs = rest[:pg], rest[pg:2 * pg], rest[2 * pg:3 * pg]
    o_ref = rest[3 * pg]
    qbd, m_scr, l_scr, acc_scr, run_scr, padk, padv = rest[3 * pg + 1:]
    s_idx = pl.program_id(1)
    R = N_HEADS * tq
    r_ = lax.broadcasted_iota(jnp.int32, (LANES, LANES), 0)
    c_ = lax.broadcasted_iota(jnp.int32, (LANES, LANES), 1)
    upper = (r_ <= c_).astype(BF)

    @pl.when(s_idx == 0)
    def _():
        _init_stats(m_scr, l_scr, acc_scr)
        qbd[...] = _blockdiag_rows(zq_ref[:, :BRANCH_W]).astype(BF)
        run_scr[...] = jnp.zeros(run_scr.shape, F32)

    def step(k_bf, v_bf, lf, mask):
        lf8 = jnp.concatenate([lf, jnp.zeros((8 - N_HEADS, LANES), F32)], axis=0)
        fk = run_scr[...] + _dot_exact01(lf8, upper)
        run_scr[...] = jnp.broadcast_to(fk[:, LANES - 1:LANES], run_scr.shape)
        s = _dot_nt(qbd[...], k_bf) * (HEAD_DIM ** -0.5) - _rows_per_head(fk, tq)
        if mask is not None:
            s = jnp.where(mask, s, NEG)
        _online(s, v_bf, m_scr, l_scr, acc_scr)

    for p in range(pg):
        step(k_refs[p][...].astype(BF), v_refs[p][...].astype(BF), lf_refs[p][...], None)

    @pl.when(s_idx == pl.num_programs(1) - 1)
    def _():
        padk[...] = jnp.zeros(padk.shape, F32)
        padv[...] = jnp.zeros(padv.shape, F32)
        padk[0:tq, :] = zq_ref[:, BRANCH_W:]
        padv[0:tq, :] = zv_ref[:, :BRANCH_W]
        row_t = lax.broadcasted_iota(jnp.int32, (R, LANES), 0) % tq
        col = lax.broadcasted_iota(jnp.int32, (R, LANES), 1)
        step(padk[...].astype(BF), padv[...].astype(BF), lfn_ref[...], col <= row_t)
        o_ref[...] = _diag_heads(acc_scr[...] / l_scr[...][:, :1], tq)


def _fox_decode(z, lf_new_t, pool_k, pool_v, pool_lft, pt_flat, l, B, tq, np_, pg):
    R = N_HEADS * tq
    grid_spec = pltpu.PrefetchScalarGridSpec(
        num_scalar_prefetch=1, grid=(B, np_ // pg),
        in_specs=[pl.BlockSpec((None, tq, CHUNK), lambda b, s, pt: (CH_FQK, b, 0)),
                  pl.BlockSpec((None, tq, CHUNK), lambda b, s, pt: (CH_FV_BQ, b, 0)),
                  pl.BlockSpec((None, N_HEADS, LANES), lambda b, s, pt: (b, 0, 0))]
        + _page_specs(pg, BRANCH_W, l, pg, np_) + _page_specs(pg, BRANCH_W, l, pg, np_)
        + [pl.BlockSpec((None, None, N_HEADS, PAGE), lambda b, s, pt, p=p: (l, pt[b * np_ + s * pg + p], 0, 0))
           for p in range(pg)],
        out_specs=pl.BlockSpec((tq, BRANCH_W), lambda b, s, pt: (b, 0)),
        scratch_shapes=[pltpu.VMEM((R, BRANCH_W), BF), pltpu.VMEM((R, LANES), F32), pltpu.VMEM((R, LANES), F32),
                        pltpu.VMEM((R, BRANCH_W), F32), pltpu.VMEM((8, LANES), F32),
                        pltpu.VMEM((PAGE, BRANCH_W), F32), pltpu.VMEM((PAGE, BRANCH_W), F32)])
    return pl.pallas_call(
        functools.partial(_fox_dec_kernel, pg=pg, tq=tq), grid_spec=grid_spec,
        out_shape=jax.ShapeDtypeStruct((B * tq, BRANCH_W), F32),
        compiler_params=_params(("arbitrary", "arbitrary")), name="fox_decode",
    )(pt_flat, z, z, lf_new_t, *([pool_k] * pg), *([pool_v] * pg), *([pool_lft] * pg))


def _mla_dec_kernel(pt_ref, ql_ref, qr_ref, cn_ref, kn_ref, wuv_ref, *rest, pg, tq):
    c_refs, r_refs = rest[:pg], rest[pg:2 * pg]
    o_ref = rest[2 * pg]
    ql_scr, qr_scr, m_scr, l_scr, acc_scr, padc, padr = rest[2 * pg + 1:]
    s_idx = pl.program_id(1)
    R = N_HEADS * tq

    @pl.when(s_idx == 0)
    def _():
        _init_stats(m_scr, l_scr, acc_scr)
        ql_scr[...] = _head_rows(ql_ref[...], MLA_KV_RANK).astype(BF)
        qr_scr[...] = _head_rows(qr_ref[...], MLA_ROPE).astype(BF)

    def step(c_bf, r_bf, mask):
        s = (_dot_nt(ql_scr[...], c_bf) + _dot_nt(qr_scr[...], r_bf)) * _MLA_SCALE
        if mask is not None:
            s = jnp.where(mask, s, NEG)
        _online(s, c_bf, m_scr, l_scr, acc_scr)

    for p in range(pg):
        step(c_refs[p][...].astype(BF), r_refs[p][...].astype(BF), None)

    @pl.when(s_idx == pl.num_programs(1) - 1)
    def _():
        padc[...] = jnp.zeros(padc.shape, F32)
        padr[...] = jnp.zeros(padr.shape, F32)
        padc[0:tq, :] = cn_ref[...]
        padr[0:tq, :] = kn_ref[:, :MLA_ROPE]
        row_t = lax.broadcasted_iota(jnp.int32, (R, LANES), 0) % tq
        col = lax.broadcasted_iota(jnp.int32, (R, LANES), 1)
        step(padc[...].astype(BF), padr[...].astype(BF), col <= row_t)
        o_lat = (acc_scr[...] / l_scr[...][:, :1]).astype(BF)
        for h in range(N_HEADS):
            o_ref[:, h * HEAD_DIM:(h + 1) * HEAD_DIM] = jnp.dot(
                o_lat[h * tq:(h + 1) * tq], wuv_ref[h], preferred_element_type=F32)


def _mla_decode(qlat, qrope, ckv, krlf, wuv, pool_c, pool_r, pt_flat, l, B, tq, np_, pg):
    R = N_HEADS * tq
    row = lambda w: pl.BlockSpec((tq, w), lambda b, s, pt: (b, 0))
    grid_spec = pltpu.PrefetchScalarGridSpec(
        num_scalar_prefetch=1, grid=(B, np_ // pg),
        in_specs=[row(512), row(128), row(128), row(128),
                  pl.BlockSpec(wuv.shape, lambda b, s, pt: (0, 0, 0))]
        + _page_specs(pg, MLA_KV_RANK, l, pg, np_) + _page_specs(pg, MLA_ROPE, l, pg, np_),
        out_specs=pl.BlockSpec((tq, BRANCH_W), lambda b, s, pt: (b, 0)),
        scratch_shapes=[pltpu.VMEM((R, MLA_KV_RANK), BF), pltpu.VMEM((R, MLA_ROPE), BF),
                        pltpu.VMEM((R, LANES), F32), pltpu.VMEM((R, LANES), F32),
                        pltpu.VMEM((R, MLA_KV_RANK), F32),
                        pltpu.VMEM((PAGE, MLA_KV_RANK), F32), pltpu.VMEM((PAGE, MLA_ROPE), F32)])
    return pl.pallas_call(
        functools.partial(_mla_dec_kernel, pg=pg, tq=tq), grid_spec=grid_spec,
        out_shape=jax.ShapeDtypeStruct((B * tq, BRANCH_W), F32),
        compiler_params=_params(("arbitrary", "arbitrary")), name="mla_decode",
    )(pt_flat, qlat, qrope, ckv, krlf, wuv, *([pool_c] * pg), *([pool_r] * pg))


def _moba_dec_kernel(pt_ref, zq_ref, zkv_ref, bias_ref, *rest, pg, tq, nb):
    k_refs, v_refs = rest[:pg], rest[pg:2 * pg]
    o_ref = rest[2 * pg]
    qbd, qf_scr, km_scr, mb_scr, lb_scr, ob_scr, padk, padv = rest[2 * pg + 1:]
    s_idx = pl.program_id(1)
    R = N_HEADS * tq
    scale = HEAD_DIM ** -0.5

    @pl.when(s_idx == 0)
    def _():
        q = _blockdiag_rows(zq_ref[:, BRANCH_W:])
        qf_scr[...] = q
        qbd[...] = q.astype(BF)
        km_scr[...] = jnp.zeros(km_scr.shape, F32)

    def block_stats(k_parts, v_parts, bias_parts, mask_parts):
        s_parts = []
        for kp, bp, mp in zip(k_parts, bias_parts, mask_parts):
            s = _dot_nt(qbd[...], kp.astype(BF)) * scale + bp
            s_parts.append(s if mp is None else jnp.where(mp, s, NEG))
        m = s_parts[0].max(axis=-1, keepdims=True)
        for s in s_parts[1:]:
            m = jnp.maximum(m, s.max(axis=-1, keepdims=True))
        lsum = jnp.zeros((R, 1), F32)
        o = jnp.zeros((R, BRANCH_W), F32)
        for s, vp, mp in zip(s_parts, v_parts, mask_parts):
            p = jnp.exp(s - m)
            if mp is not None:
                p = jnp.where(mp, p, 0.0)
            lsum = lsum + jnp.sum(p, axis=-1, keepdims=True)
            o = o + jnp.dot(p.astype(BF), vp.astype(BF), preferred_element_type=F32)
        return m, lsum, o

    for j in range(pg // 2):
        n = s_idx * (pg // 2) + j
        ka, kb = k_refs[2 * j][...], k_refs[2 * j + 1][...]
        va, vb = v_refs[2 * j][...], v_refs[2 * j + 1][...]
        km_scr[pl.ds(n, 1), :] = (jnp.sum(ka, axis=0, keepdims=True) + jnp.sum(kb, axis=0, keepdims=True)) \
            * (1.0 / MOBA_BLOCK)
        last = n == nb - 1
        b_far = bias_ref[2 * R:3 * R, :]
        b_last = bias_ref[R:2 * R, :]
        bias = jnp.where(last, b_last, b_far)
        m, lsum, o = block_stats([ka, kb], [va, vb], [bias[:, :PAGE], bias[:, PAGE:]], [None, None])
        mb_scr[n] = jnp.broadcast_to(m, (R, LANES))
        lb_scr[n] = jnp.broadcast_to(lsum, (R, LANES))
        ob_scr[n] = o

    @pl.when(s_idx == pl.num_programs(1) - 1)
    def _():
        padk[...] = jnp.zeros(padk.shape, F32)
        padv[...] = jnp.zeros(padv.shape, F32)
        padk[0:tq, :] = zkv_ref[:, :BRANCH_W]
        padv[0:tq, :] = zkv_ref[:, BRANCH_W:]
        row_t = lax.broadcasted_iota(jnp.int32, (R, LANES), 0) % tq
        col = lax.broadcasted_iota(jnp.int32, (R, LANES), 1)
        own_mask = col <= row_t
        m_tot, l_tot, o_tot = block_stats([padk[...]], [padv[...]], [bias_ref[0:R, :PAGE]], [own_mask])
        g = _dot_f32_nt(qf_scr[...], km_scr[...])
        sel = _top_blocks(g, nb, nb)
        for n in range(nb):
            m_n = jnp.where(sel[:, n:n + 1] > 0.0, mb_scr[n][:, :1], NEG)
            m_new = jnp.maximum(m_tot, m_n)
            a_old = jnp.exp(m_tot - m_new)
            a_n = jnp.where(sel[:, n:n + 1] > 0.0, jnp.exp(m_n - m_new), 0.0)
            l_tot = a_old * l_tot + a_n * lb_scr[n][:, :1]
            o_tot = a_old * o_tot + a_n * ob_scr[n]
            m_tot = m_new
        o_ref[...] = _diag_heads(o_tot / l_tot, tq)


def _moba_decode(z, bias, pool_k, pool_v, pt_flat, l, B, tq, np_, pg):
    R = N_HEADS * tq
    nb = np_ * PAGE // MOBA_BLOCK
    assert pg % 2 == 0 and nb <= LANES
    grid_spec = pltpu.PrefetchScalarGridSpec(
        num_scalar_prefetch=1, grid=(B, np_ // pg),
        in_specs=[pl.BlockSpec((None, tq, CHUNK), lambda b, s, pt: (CH_FV_BQ, b, 0)),
                  pl.BlockSpec((None, tq, CHUNK), lambda b, s, pt: (CH_BKV, b, 0)),
                  pl.BlockSpec(bias.shape, lambda b, s, pt: (0, 0))]
        + _page_specs(pg, BRANCH_W, l, pg, np_) + _page_specs(pg, BRANCH_W, l, pg, np_),
        out_specs=pl.BlockSpec((tq, BRANCH_W), lambda b, s, pt: (b, 0)),
        scratch_shapes=[pltpu.VMEM((R, BRANCH_W), BF), pltpu.VMEM((R, BRANCH_W), F32),
                        pltpu.VMEM((LANES, BRANCH_W), F32),
                        pltpu.VMEM((nb, R, LANES), F32), pltpu.VMEM((nb, R, LANES), F32),
                        pltpu.VMEM((nb, R, BRANCH_W), F32),
                        pltpu.VMEM((PAGE, BRANCH_W), F32), pltpu.VMEM((PAGE, BRANCH_W), F32)])
    return pl.pallas_call(
        functools.partial(_moba_dec_kernel, pg=pg, tq=tq, nb=nb), grid_spec=grid_spec,
        out_shape=jax.ShapeDtypeStruct((B * tq, BRANCH_W), F32),
        compiler_params=_params(("arbitrary", "arbitrary")), name="moba_decode",
    )(pt_flat, z, z, bias, *([pool_k] * pg), *([pool_v] * pg))


_HALO = 32


def _conv_kernel(z_ref, hist_ref, w_ref, b_ref, g_ref, beta_ref, y_ref, buf_ref, ext, *, tt):
    ti = pl.program_id(1)

    @pl.when(ti == 0)
    def _():
        ext[0:_HALO, :] = hist_ref[0]

    @pl.when(ti > 0)
    def _():
        ext[0:_HALO, :] = ext[tt:tt + _HALO, :]

    z = z_ref[...]
    ext[_HALO:_HALO + tt, :] = z[:, :BRANCH_W] * jax.nn.sigmoid(z[:, BRANCH_W:])
    first = _HALO - (CONV_W - 1)
    y = jnp.zeros((tt, BRANCH_W), F32)
    for j in range(CONV_W):
        y = y + w_ref[j:j + 1, :] * ext[first + j:first + j + tt, :]
    y = y + b_ref[...]
    mu = jnp.mean(y, axis=-1, keepdims=True)
    var = jnp.mean(jnp.square(y - mu), axis=-1, keepdims=True)
    yn = (y - mu) * lax.rsqrt(var + EPS) * g_ref[...] + beta_ref[...]
    y_ref[...] = yn * jax.nn.sigmoid(yn)

    @pl.when(ti == pl.num_programs(1) - 1)
    def _():
        buf_ref[0] = ext[tt + first:tt + _HALO, :]


def _conv(z, hist32, w_dw, b_dw, ln_g, ln_b, B, T, tt):
    nt = T // tt
    vec = lambda: pl.BlockSpec((1, BRANCH_W), lambda b, t: (0, 0))
    return pl.pallas_call(
        functools.partial(_conv_kernel, tt=tt), grid=(B, nt),
        in_specs=[pl.BlockSpec((None, tt, CHUNK), lambda b, t: (CH_CONV, b * nt + t, 0)),
                  pl.BlockSpec((1, _HALO, BRANCH_W), lambda b, t: (b, 0, 0)),
                  pl.BlockSpec((CONV_W, BRANCH_W), lambda b, t: (0, 0)), vec(), vec(), vec()],
        out_specs=[pl.BlockSpec((tt, BRANCH_W), lambda b, t: (b * nt + t, 0)),
                   pl.BlockSpec((1, CONV_W - 1, BRANCH_W), lambda b, t: (b, 0, 0))],
        out_shape=[jax.ShapeDtypeStruct((B * T, BRANCH_W), F32),
                   jax.ShapeDtypeStruct((B, CONV_W - 1, BRANCH_W), F32)],
        scratch_shapes=[pltpu.VMEM((tt + _HALO, BRANCH_W), F32)],
        compiler_params=_params(("arbitrary", "arbitrary")), name="conv_module",
    )(z, hist32, w_dw, b_dw.reshape(1, -1), ln_g.reshape(1, -1), ln_b.reshape(1, -1))


def _merge_kernel(x_ref, g1_ref, gate_ref, of_ref, om_ref, ob_ref, oc_ref, wb_ref, wo_ref, o_ref):
    merged = None
    for n, br in enumerate((of_ref, om_ref, ob_ref, oc_ref)):
        proj = jnp.dot(br[...].astype(BF), wb_ref[n], preferred_element_type=F32)
        sig = jnp.concatenate([gate_ref[2 * n], gate_ref[2 * n + 1]], axis=-1)
        merged = sig * proj if merged is None else merged + sig * proj
    out = jnp.dot(merged.astype(BF), wo_ref[...], preferred_element_type=F32)
    o_ref[...] = x_ref[...] + g1_ref[...] * out.reshape(x_ref.shape)


def _merge(tok, x, mod, l, z, o_fox, o_mla, o_moba, y_conv, w_branch, w_out):
    return pl.pallas_call(
        _merge_kernel, grid=(tok.n_tiles,),
        in_specs=[tok.x_spec(), tok.mod_spec(l, 2), tok.z_spec(0, 8)]
        + [tok.flat_spec(BRANCH_W)] * 4
        + [pl.BlockSpec(w_branch.shape, lambda i: (0, 0, 0)), pl.BlockSpec(w_out.shape, lambda i: (0, 0))],
        out_specs=tok.x_spec(),
        out_shape=jax.ShapeDtypeStruct(x.shape, F32),
        compiler_params=_params(("arbitrary",)), name="merge",
    )(x, mod, z, o_fox, o_mla, o_moba, y_conv, w_branch, w_out)


def _ffn_kernel(x_ref, sh_ref, sc_ref, g2_ref, gn_ref, wr_ref, br_ref, wg_ref, wu_ref, wd_ref, gout_ref,
                o_ref, h_scr, comb_scr, acc_e, acc_o, *, n_exp, final):
    e, f = pl.program_id(1), pl.program_id(2)
    nf = pl.num_programs(2)
    tm = h_scr.shape[0]

    @pl.when((e == 0) & (f == 0))
    def _():
        h = _modulated_norm(x_ref[...], gn_ref[...], sc_ref[...], sh_ref[...]).reshape(h_scr.shape)
        h_scr[...] = h.astype(BF)
        acc_o[...] = jnp.zeros(acc_o.shape, F32)
        if n_exp > 1:
            lane = lax.broadcasted_iota(jnp.int32, (tm, LANES), 1)
            logits = _dot_f32_nt(h, wr_ref[...]) + br_ref[...]
            logits = jnp.where(lane < n_exp, logits, NEG)
            m1 = jnp.max(logits, axis=-1, keepdims=True)
            i1 = jnp.min(jnp.where(logits == m1, lane, LANES), axis=-1, keepdims=True)
            rest = jnp.where(lane == i1, NEG, logits)
            m2 = jnp.max(rest, axis=-1, keepdims=True)
            i2 = jnp.min(jnp.where(rest == m2, lane, LANES), axis=-1, keepdims=True)
            e2 = jnp.exp(m2 - m1)
            w1 = 1.0 / (1.0 + e2)
            w2 = e2 / (1.0 + e2)
            comb_scr[...] = jnp.where(lane == i1, w1, 0.0) + jnp.where(lane == i2, w2, 0.0)

    @pl.when(f == 0)
    def _():
        acc_e[...] = jnp.zeros(acc_e.shape, F32)

    h = h_scr[...]
    a = jnp.dot(h, wg_ref[0], preferred_element_type=F32)
    u = jnp.dot(h, wu_ref[0], preferred_element_type=F32)
    act = (a * jax.nn.sigmoid(a) * u).astype(BF)
    acc_e[...] += jnp.dot(act, wd_ref[0], preferred_element_type=F32)

    @pl.when(f == nf - 1)
    def _():
        if n_exp > 1:
            lane = lax.broadcasted_iota(jnp.int32, (tm, LANES), 1)
            w = jnp.sum(jnp.where(lane == e, comb_scr[...], 0.0), axis=-1, keepdims=True)
            acc_o[...] += w * acc_e[...]
        else:
            acc_o[...] += acc_e[...]

    @pl.when((e == n_exp - 1) & (f == nf - 1))
    def _():
        xn = x_ref[...] + g2_ref[...] * acc_o[...].reshape(x_ref.shape)
        if final:
            ms = jnp.mean(xn * xn, axis=-1, keepdims=True)
            xn = xn * lax.rsqrt(ms + EPS) * gout_ref[...]
        o_ref[...] = xn


def _ffn(tok, x, mod, l, g_norm, w_router_t, b_router, wg, wu, wd, g_out, tf, final):
    n_exp, _, F = wg.shape
    nf = F // tf
    assert F % tf == 0 and tf % LANES == 0
    return pl.pallas_call(
        functools.partial(_ffn_kernel, n_exp=n_exp, final=final), grid=(tok.n_tiles, n_exp, nf),
        in_specs=[tok.x_spec(), tok.mod_spec(l, 3), tok.mod_spec(l, 4), tok.mod_spec(l, 5),
                  pl.BlockSpec((1, D_MODEL), lambda i, e, f: (0, 0)),
                  pl.BlockSpec(w_router_t.shape, lambda i, e, f: (0, 0)),
                  pl.BlockSpec((1, LANES), lambda i, e, f: (0, 0)),
                  pl.BlockSpec((1, D_MODEL, tf), lambda i, e, f: (e, 0, f)),
                  pl.BlockSpec((1, D_MODEL, tf), lambda i, e, f: (e, 0, f)),
                  pl.BlockSpec((1, tf, D_MODEL), lambda i, e, f: (e, f, 0)),
                  pl.BlockSpec((1, D_MODEL), lambda i, e, f: (0, 0))],
        out_specs=tok.x_spec(),
        out_shape=jax.ShapeDtypeStruct(x.shape, F32),
        scratch_shapes=[pltpu.VMEM((tok.tm, D_MODEL), BF), pltpu.VMEM((tok.tm, LANES), F32),
                        pltpu.VMEM((tok.tm, D_MODEL), F32), pltpu.VMEM((tok.tm, D_MODEL), F32)],
        compiler_params=_params(("arbitrary", "arbitrary", "arbitrary")), name="ffn",
    )(x, mod, mod, mod, g_norm.reshape(1, D_MODEL), w_router_t, b_router, wg, wu, wd, g_out.reshape(1, D_MODEL))


def _largest_tile(n, cap):
    t = min(n, cap)
    while n % t:
        t //= 2
    return t


def _ffn_tile(F):
    for tf in (512, 1408, 896, 256, 128):
        if F % tf == 0:
            return tf
    raise ValueError(F)


def kernel(x_prompt, x_sample, cache_fox_k, cache_fox_v, cache_fox_logf, cache_mla_ckv, cache_mla_krope,
           cache_moba_k, cache_moba_v, state_conv, page_table, c_prompt, c_sample,
           w_ada, b_ada, norm_mix, norm_ffn, w_in, b_fox_f, g_mla_q, g_mla_kv, w_mla_uq, w_mla_ukv,
           w_dw, b_dw, conv_ln_g, conv_ln_b, w_branch, w_out, rel_bias,
           ffn_w_gate, ffn_w_up, ffn_w_down, moe_w_router, moe_b_router, moe_w_gate, moe_w_up, moe_w_down,
           norm_out):
    Bp, Tp, D = x_prompt.shape
    Bs, Ts, _ = x_sample.shape
    L = w_ada.shape[0]
    n_pool = cache_fox_k.shape[1]
    np_ = page_table.shape[1]
    past_len = np_ * PAGE
    assert D == D_MODEL and Tp % MOBA_BLOCK == 0 and past_len % MOBA_BLOCK == 0 and Ts == 8
    pg = _largest_tile(np_, 8)

    n_c = Bp + Bs
    mc = -(-n_c // 8) * 8
    c_all = jnp.concatenate([c_prompt, c_sample, jnp.zeros((mc - n_c, D), F32)], axis=0)
    mod_all = _ada(c_all, w_ada, b_ada)

    def mod_view(lo, n):
        return mod_all[:, lo:lo + n].reshape(L, n, 6, D).transpose(0, 2, 1, 3)[:, :, :, None, :]

    mod_p, mod_s = mod_view(0, Bp), mod_view(Bp, Bs)

    tok_p = _Tok(Bp, Tp, 1, _largest_tile(Tp, 512))
    tok_s = _Tok(Bs, Ts, _largest_tile(Bs, 64), Ts)
    tq_p = _largest_tile(Tp, 512)

    rope_p = _rope_tables(jnp.arange(Tp), 1)
    rope_s = _rope_tables(past_len + jnp.arange(Ts), tok_s.bb)

    tab = rel_bias.T.astype(F32)
    blk = MOBA_BLOCK
    tab_rows_p = jnp.tile(jnp.repeat(tab, blk, axis=0), (3, 1))
    qpos_p = (jnp.arange(3)[:, None, None] * blk + jnp.arange(blk)[None, None, :]
              + jnp.zeros((1, N_HEADS, 1), jnp.int32)).reshape(-1, 1).astype(jnp.int32)
    bias_p = _relbias(tab_rows_p, qpos_p, jnp.arange(blk, dtype=jnp.int32)[None, :])
    bias_p = bias_p.reshape(3, N_HEADS, blk, blk)
    tab_rows_s = jnp.tile(jnp.repeat(tab, Ts, axis=0), (3, 1))
    qpos_s = (jnp.arange(3)[:, None, None] * blk + jnp.arange(Ts)[None, None, :]
              + jnp.zeros((1, N_HEADS, 1), jnp.int32)).reshape(-1, 1).astype(jnp.int32)
    bias_s = _relbias(tab_rows_s, qpos_s, jnp.arange(blk, dtype=jnp.int32)[None, :])

    pt_flat = page_table.reshape(-1).astype(jnp.int32)
    pool4 = lambda a, w: a.reshape(L, n_pool, PAGE, w)
    pk_fox, pv_fox = pool4(cache_fox_k, BRANCH_W), pool4(cache_fox_v, BRANCH_W)
    pk_moba, pv_moba = pool4(cache_moba_k, BRANCH_W), pool4(cache_moba_v, BRANCH_W)
    p_lft = jnp.swapaxes(cache_fox_logf, 2, 3)
    hist_p = jnp.zeros((Bp, _HALO, BRANCH_W), F32)
    hist_s = jnp.pad(state_conv, ((0, 0), (0, 0), (_HALO - (CONV_W - 1), 0), (0, 0)))

    xp, xs = x_prompt, x_sample
    rows_p, rows_s = [], []
    for l in range(L):
        w_perm = _perm_w_in(w_in[l])
        w_uq = w_mla_uq[l]
        wn = w_uq[:, :, :MLA_NOPE].reshape(MLA_Q_RANK, -1).astype(BF)
        wr = w_uq[:, :, MLA_NOPE:].reshape(MLA_Q_RANK, -1).astype(BF)
        wuk = jnp.transpose(w_mla_ukv[l][:, :, :MLA_NOPE], (1, 2, 0)).astype(BF)
        wuv = jnp.transpose(w_mla_ukv[l][:, :, MLA_NOPE:], (1, 0, 2)).astype(BF)
        bf128 = jnp.zeros((1, LANES), F32).at[0, MLA_ROPE:MLA_ROPE + N_HEADS].set(b_fox_f[l])
        gq, gkv = g_mla_q[l].reshape(1, -1), g_mla_kv[l].reshape(1, -1)
        wb, wo = w_branch[l].astype(BF), w_out[l].astype(BF)
        i = l // 2
        if l % 2 == 0:
            wg, wu, wd = ffn_w_gate[i][None].astype(BF), ffn_w_up[i][None].astype(BF), ffn_w_down[i][None].astype(BF)
            w_rt = jnp.zeros((LANES, D), F32)
            b_rt = jnp.zeros((1, LANES), F32)
        else:
            wg, wu, wd = moe_w_gate[i].astype(BF), moe_w_up[i].astype(BF), moe_w_down[i].astype(BF)
            n_e = wg.shape[0]
            w_rt = jnp.zeros((LANES, D), F32).at[:n_e].set(moe_w_router[i].T)
            b_rt = jnp.zeros((1, LANES), F32).at[0, :n_e].set(moe_b_router[i])
        tf = _ffn_tile(wg.shape[2])
        final = l == L - 1

        z = _inproj(tok_p, xp, mod_p, l, norm_mix[l], w_perm)
        qlat, qrope, ckv, krlf = _prep(tok_p, z, gq, gkv, wn, wr, wuk, bf128, rope_p)
        logf = krlf[:, MLA_ROPE:MLA_ROPE + N_HEADS].reshape(Bp, Tp, N_HEADS)
        cum_t = _cumsum_time(jnp.swapaxes(logf, 1, 2).reshape(Bp * N_HEADS, Tp // LANES, LANES))
        cum_t = cum_t.reshape(Bp, N_HEADS, Tp)
        o_fox = _fox_prompt(z, jnp.swapaxes(cum_t, 1, 2), cum_t, Bp, Tp, tq_p)
        o_mla = _mla_prompt(qlat, qrope, ckv, krlf, wuv, Bp, Tp, tq_p)
        kmean = _kmean_prompt(z, Bp * Tp).reshape(Bp, Tp // blk, BRANCH_W)
        o_moba = _moba_prompt(z, kmean, bias_p, Bp, Tp)
        y_conv, buf_p = _conv(z, hist_p, w_dw[l], b_dw[l], conv_ln_g[l], conv_ln_b[l], Bp, Tp, tok_p.tt)
        xp = _merge(tok_p, xp, mod_p, l, z, o_fox, o_mla, o_moba, y_conv, wb, wo)
        xp = _ffn(tok_p, xp, mod_p, l, norm_ffn[l], w_rt, b_rt, wg, wu, wd, norm_out, tf, final)
        heads = lambda a: a.reshape(Bp, Tp, N_HEADS, HEAD_DIM)
        rows_p.append((heads(z[CH_FQK, :, BRANCH_W:]), heads(z[CH_FV_BQ, :, :BRANCH_W]), logf,
                       ckv.reshape(Bp, Tp, MLA_KV_RANK), krlf[:, :MLA_ROPE].reshape(Bp, Tp, MLA_ROPE),
                       heads(z[CH_BKV, :, :BRANCH_W]), heads(z[CH_BKV, :, BRANCH_W:]), buf_p))

        z = _inproj(tok_s, xs, mod_s, l, norm_mix[l], w_perm)
        qlat, qrope, ckv, krlf = _prep(tok_s, z, gq, gkv, wn, wr, wuk, bf128, rope_s)
        logf = krlf[:, MLA_ROPE:MLA_ROPE + N_HEADS].reshape(Bs, Ts, N_HEADS)
        lf_new_t = jnp.pad(jnp.swapaxes(logf, 1, 2), ((0, 0), (0, 0), (0, LANES - Ts)))
        o_fox = _fox_decode(z, lf_new_t, pk_fox, pv_fox, p_lft, pt_flat, l, Bs, Ts, np_, pg)
        o_mla = _mla_decode(qlat, qrope, ckv, krlf, wuv, cache_mla_ckv, cache_mla_krope, pt_flat, l, Bs, Ts, np_, pg)
        o_moba = _moba_decode(z, bias_s, pk_moba, pv_moba, pt_flat, l, Bs, Ts, np_, pg)
        y_conv, buf_s = _conv(z, hist_s[l], w_dw[l], b_dw[l], conv_ln_g[l], conv_ln_b[l], Bs, Ts, Ts)
        xs = _merge(tok_s, xs, mod_s, l, z, o_fox, o_mla, o_moba, y_conv, wb, wo)
        xs = _ffn(tok_s, xs, mod_s, l, norm_ffn[l], w_rt, b_rt, wg, wu, wd, norm_out, tf, final)
        heads = lambda a: a.reshape(Bs, Ts, N_HEADS, HEAD_DIM)
        rows_s.append((heads(z[CH_FQK, :, BRANCH_W:]), heads(z[CH_FV_BQ, :, :BRANCH_W]), logf,
                       ckv.reshape(Bs, Ts, MLA_KV_RANK), krlf[:, :MLA_ROPE].reshape(Bs, Ts, MLA_ROPE),
                       heads(z[CH_BKV, :, :BRANCH_W]), heads(z[CH_BKV, :, BRANCH_W:]), buf_s))

    stack = lambda rows: [jnp.stack([r[j] for r in rows]) for j in range(8)]
    return (xp, xs, *stack(rows_p), *stack(rows_s))
```

```python
import functools
import math

import numpy as np
import jax
import jax.numpy as jnp
from jax import lax
from jax.experimental import pallas as pl
from jax.experimental.pallas import tpu as pltpu

F32 = jnp.float32
BF = jnp.bfloat16

D_MODEL = 1024
N_HEADS = 4
HEAD_DIM = 64
BRANCH_W = 256
MLA_Q_RANK = 256
MLA_KV_RANK = 128
MLA_NOPE = 64
MLA_ROPE = 32
CONV_W = 31
MOBA_BLOCK = 256
MOBA_TOPK = 3
REL_BUCKETS = 32
REL_MAX_DIST = 128
ROPE_THETA = 10000.0
EPS = 1e-6
PAGE = 128
IN_SIZES = (256, 256, 256, 4, 256, 128, 32, 256, 256, 256, 256, 256, 4096)

LANES = 128
CHUNK = 512
N_CHUNKS = 13
CH_FQK, CH_FV_BQ, CH_BKV, CH_MLA, CH_CONV = 8, 9, 10, 11, 12
NEG = -1e30
VMEM_LIMIT = 56 * 2 ** 20
QK_SCALE = HEAD_DIM ** -0.5


def _params(sem):
    return pltpu.CompilerParams(dimension_semantics=sem, vmem_limit_bytes=VMEM_LIMIT)


def _split3(x):
    x1 = x.astype(BF)
    r1 = x - x1.astype(F32)
    x2 = r1.astype(BF)
    x3 = (r1 - x2.astype(F32)).astype(BF)
    return x1, x2, x3


def _dot_exact01(x, m01):
    acc = None
    for piece in _split3(x):
        t = jnp.dot(piece, m01, preferred_element_type=F32)
        acc = t if acc is None else acc + t
    return acc


def _dot01_exact(m01, x):
    acc = None
    for piece in _split3(x):
        t = jnp.dot(m01, piece, preferred_element_type=F32)
        acc = t if acc is None else acc + t
    return acc


def _dot_f32(a, b, dn):
    a1, a2, a3 = _split3(a)
    b1, b2, b3 = _split3(b)
    acc = None
    for x, y in ((a1, b1), (a1, b2), (a2, b1), (a2, b2), (a1, b3), (a3, b1)):
        t = lax.dot_general(x, y, dn, preferred_element_type=F32)
        acc = t if acc is None else acc + t
    return acc


_NT = (((1,), (1,)), ((), ()))
_NN = (((1,), (0,)), ((), ()))


def _dot_nt(a, b):
    return lax.dot_general(a, b, _NT, preferred_element_type=F32)


def _log_sigmoid(x):
    return jnp.minimum(x, 0.0) - jnp.log1p(jnp.exp(-jnp.abs(x)))


def _rel_thresholds():
    max_exact = REL_BUCKETS // 2
    n = np.arange(0, 4 * REL_MAX_DIST)
    nf = np.maximum(n, 1).astype(np.float64)
    large = max_exact + (np.log(nf / max_exact) / math.log(REL_MAX_DIST / max_exact)
                         * (REL_BUCKETS - max_exact)).astype(np.int64)
    bucket = np.where(n < max_exact, n, np.minimum(large, REL_BUCKETS - 1))
    return [int(np.argmax(bucket >= k)) for k in range(1, REL_BUCKETS)]


_REL_THR = _rel_thresholds()


class _Tok:
    def __init__(self, B, T, bb, tt):
        assert B % bb == 0 and T % tt == 0 and (bb == 1 or tt == T) and tt % 8 == 0
        self.B, self.T, self.bb, self.tt = B, T, bb, tt
        self.nt = T // tt
        self.n_tiles = (B // bb) * self.nt
        self.tm = bb * tt
        self.N = B * T

    def x_spec(self):
        nt = self.nt
        return pl.BlockSpec((self.bb, self.tt, D_MODEL), lambda i, *_: (i // nt, i % nt, 0))

    def mod_spec(self, l, k):
        nt = self.nt
        return pl.BlockSpec((None, None, self.bb, 1, D_MODEL), lambda i, *_: (l, k, i // nt, 0, 0))

    def flat_spec(self, width):
        return pl.BlockSpec((self.tm, width), lambda i, *_: (i, 0))

    def z_spec(self, chunk, n=None):
        if n is None:
            return pl.BlockSpec((None, self.tm, CHUNK), lambda i, *_: (chunk, i, 0))
        return pl.BlockSpec((n, self.tm, CHUNK), lambda i, *_: (chunk // n, i, 0))


def _ada_kernel(c_ref, w_ref, b_ref, o_ref):
    c = c_ref[...]
    s = (c * jax.nn.sigmoid(c)).astype(BF)
    o_ref[...] = jnp.dot(s, w_ref[0].astype(BF), preferred_element_type=F32) + b_ref[0]


def _ada(c_all, w_ada, b_ada):
    L, D, D6 = w_ada.shape
    Mc = c_all.shape[0]
    tn = CHUNK
    per = D // tn
    return pl.pallas_call(
        _ada_kernel, grid=(L, D6 // tn),
        in_specs=[pl.BlockSpec((Mc, D), lambda l, j: (0, 0)),
                  pl.BlockSpec((1, D, tn), lambda l, j: (l, 0, j)),
                  pl.BlockSpec((1, 1, tn), lambda l, j: (l, 0, j))],
        out_specs=pl.BlockSpec((None, None, Mc, tn), lambda l, j: (l, j // per, 0, j % per)),
        out_shape=jax.ShapeDtypeStruct((L, D6 // D, Mc, D), F32),
        compiler_params=_params(("arbitrary", "arbitrary")), name="ada",
    )(c_all, w_ada, b_ada.reshape(L, 1, D6))


def _modulated_norm(x, g, sc, sh):
    ms = jnp.mean(x * x, axis=-1, keepdims=True)
    y = x * lax.rsqrt(ms + EPS) * g
    return y * (1.0 + sc) + sh


def _inproj_kernel(x_ref, sh_ref, sc_ref, g_ref, wt_ref, z_ref, h_scr):
    j = pl.program_id(1)

    @pl.when(j == 0)
    def _():
        h = _modulated_norm(x_ref[...], g_ref[...], sc_ref[...], sh_ref[...])
        h_scr[...] = h.reshape(h_scr.shape).astype(BF)

    acc = _dot_nt(h_scr[...], wt_ref[...])

    @pl.when(j < 8)
    def _():
        z_ref[...] = jax.nn.sigmoid(acc)

    @pl.when(j >= 8)
    def _():
        z_ref[...] = acc


def _inproj(tok, x, mod, l, g_norm, wt_perm):
    return pl.pallas_call(
        _inproj_kernel, grid=(tok.n_tiles, N_CHUNKS),
        in_specs=[tok.x_spec(), tok.mod_spec(l, 0), tok.mod_spec(l, 1),
                  pl.BlockSpec((1, D_MODEL), lambda i, j: (0, 0)),
                  pl.BlockSpec((CHUNK, D_MODEL), lambda i, j: (j, 0))],
        out_specs=pl.BlockSpec((None, tok.tm, CHUNK), lambda i, j: (j, i, 0)),
        out_shape=jax.ShapeDtypeStruct((N_CHUNKS, tok.N, CHUNK), F32),
        scratch_shapes=[pltpu.VMEM((tok.tm, D_MODEL), BF)],
        compiler_params=_params(("arbitrary", "arbitrary")), name="inproj",
    )(x, mod, mod, g_norm.reshape(1, D_MODEL), wt_perm)


def _perm_w_in_t(w):
    wt = w.T
    offs = np.concatenate([[0], np.cumsum(IN_SIZES)])
    fq, fk, fv, ff, mcq, mckv, mkr, bq, bk, bv, cu, cg, gate = [wt[offs[i]:offs[i + 1]] for i in range(13)]
    pad = jnp.zeros((LANES - MLA_ROPE - N_HEADS, w.shape[0]), w.dtype)
    return jnp.concatenate([gate, fq, fk, fv, bq, bk, bv, mcq, mckv, mkr, ff, pad, cu, cg], axis=0).astype(BF)


def _rope_lanes(x, c, sa, sb):
    return x * c + pltpu.roll(x, LANES - MLA_ROPE // 2, 1) * sa + pltpu.roll(x, MLA_ROPE // 2, 1) * sb


def _prep_kernel(z_ref, gq_ref, gkv_ref, wn_ref, wr_ref, wuk_ref, bf_ref, c_ref, sa_ref, sb_ref,
                 qlat_ref, qrope_ref, ckv_ref, krlf_ref):
    z = z_ref[...]
    mcq, mckv, kf = z[:, :256], z[:, 256:384], z[:, 384:512]
    qn = (mcq * lax.rsqrt(jnp.mean(mcq * mcq, axis=-1, keepdims=True) + EPS) * gq_ref[...]).astype(BF)
    q_nope = jnp.dot(qn, wn_ref[...], preferred_element_type=F32)
    q_rope = jnp.dot(qn, wr_ref[...], preferred_element_type=F32)
    c, sa, sb = c_ref[...], sa_ref[...], sb_ref[...]
    qrope_ref[...] = _rope_lanes(q_rope, c, sa, sb)
    for h in range(N_HEADS):
        qh = q_nope[:, h * MLA_NOPE:(h + 1) * MLA_NOPE].astype(BF)
        qlat_ref[:, h * MLA_KV_RANK:(h + 1) * MLA_KV_RANK] = jnp.dot(qh, wuk_ref[h], preferred_element_type=F32)
    ckv_ref[...] = mckv * lax.rsqrt(jnp.mean(mckv * mckv, axis=-1, keepdims=True) + EPS) * gkv_ref[...]
    lane = lax.broadcasted_iota(jnp.int32, kf.shape, 1)
    roped = _rope_lanes(kf, c, sa, sb)
    logf = _log_sigmoid(kf + bf_ref[...])
    krlf_ref[...] = jnp.where(lane < MLA_ROPE, roped, jnp.where(lane < MLA_ROPE + N_HEADS, logf, 0.0))


def _prep(tok, z, gq, gkv, wn, wr, wuk, bf128, rope_tabs):
    N = tok.N
    full = lambda a: pl.BlockSpec(a.shape, lambda i: (0,) * a.ndim)
    tab_spec = pl.BlockSpec((tok.tm, LANES), lambda i: (i % max(tok.nt, 1), 0)) if tok.bb == 1 else \
        pl.BlockSpec((tok.tm, LANES), lambda i: (0, 0))
    c, sa, sb = rope_tabs
    return pl.pallas_call(
        _prep_kernel, grid=(tok.n_tiles,),
        in_specs=[tok.z_spec(CH_MLA), full(gq), full(gkv), full(wn), full(wr), full(wuk), full(bf128),
                  tab_spec, tab_spec, tab_spec],
        out_specs=[tok.flat_spec(512), tok.flat_spec(128), tok.flat_spec(128), tok.flat_spec(128)],
        out_shape=[jax.ShapeDtypeStruct((N, 512), F32), jax.ShapeDtypeStruct((N, 128), F32),
                   jax.ShapeDtypeStruct((N, 128), F32), jax.ShapeDtypeStruct((N, 128), F32)],
        compiler_params=_params(("arbitrary",)), name="mla_prep",
    )(z, gq, gkv, wn, wr, wuk, bf128, c, sa, sb)


def _rope_tables(pos, reps):
    half = MLA_ROPE // 2
    inv = ROPE_THETA ** (-jnp.arange(half, dtype=F32) / half)
    ang = pos.astype(F32)[:, None] * inv[None, :]
    cos, sin = jnp.cos(ang), jnp.sin(ang)
    zero = jnp.zeros_like(sin)
    tile = lambda a, b: jnp.tile(jnp.concatenate([a, b], axis=1), (reps, LANES // MLA_ROPE))
    return tile(cos, cos), tile(-sin, zero), tile(zero, sin)


def _upper_tri():
    r = lax.broadcasted_iota(jnp.int32, (LANES, LANES), 0)
    c = lax.broadcasted_iota(jnp.int32, (LANES, LANES), 1)
    return (r <= c).astype(BF)


def _cumsum_kernel(x_ref, o_ref):
    x = x_ref[0]
    R = x.shape[0]
    within = _dot_exact01(x, _upper_tri())
    tot = jnp.broadcast_to(within[:, LANES - 1:LANES], (R, LANES))
    rr = lax.broadcasted_iota(jnp.int32, (R, R), 0)
    cc = lax.broadcasted_iota(jnp.int32, (R, R), 1)
    o_ref[0] = within + _dot01_exact((cc < rr).astype(BF), tot)


def _cumsum_time(x):
    G, R, _ = x.shape
    return pl.pallas_call(
        _cumsum_kernel, grid=(G,),
        in_specs=[pl.BlockSpec((1, R, LANES), lambda g: (g, 0, 0))],
        out_specs=pl.BlockSpec((1, R, LANES), lambda g: (g, 0, 0)),
        out_shape=jax.ShapeDtypeStruct(x.shape, F32),
        compiler_params=_params(("arbitrary",)), name="fox_cumsum",
    )(x)


def _online(s, pv_fn, m_ref, l_ref, acc_ref, row_on=None, shift=None):
    m_prev = m_ref[...][:, :1]
    m_tile = jnp.max(s, axis=-1, keepdims=True)
    if shift is not None:
        m_tile = m_tile + shift
    m_new = jnp.maximum(m_prev, m_tile)
    if row_on is not None:
        m_new = jnp.where(row_on, m_new, m_prev)
    alpha = jnp.exp(m_prev - m_new)
    m_use = m_new if shift is None else m_new - shift
    if row_on is not None:
        m_use = jnp.where(row_on, m_use, -NEG)
    p = jnp.exp(s - m_use)
    l_new = alpha * l_ref[...][:, :1] + jnp.sum(p, axis=-1, keepdims=True)
    l_ref[...] = jnp.broadcast_to(l_new, l_ref.shape)
    m_ref[...] = jnp.broadcast_to(m_new, m_ref.shape)
    acc_ref[...] = alpha * acc_ref[...] + pv_fn(p.astype(BF))


def _init_stats(m_scr, l_scr, acc_scr):
    m_scr[...] = jnp.full(m_scr.shape, NEG, F32)
    l_scr[...] = jnp.zeros(l_scr.shape, F32)
    acc_scr[...] = jnp.zeros(acc_scr.shape, F32)


def _causal(tq):
    row = lax.broadcasted_iota(jnp.int32, (tq, tq), 0)
    col = lax.broadcasted_iota(jnp.int32, (tq, tq), 1)
    return col <= row


def _pv(v_bf):
    return lambda p: jnp.dot(p, v_bf, preferred_element_type=F32)


def _fox_kernel(q_ref, k_ref, v_ref, cq_ref, ck_ref, o_ref, m_scr, l_scr, acc_scr, *, tq):
    qi, ki = pl.program_id(1), pl.program_id(2)

    @pl.when(ki == 0)
    def _():
        _init_stats(m_scr, l_scr, acc_scr)

    def process(mask):
        q = q_ref[:, :BRANCH_W] * QK_SCALE
        k = k_ref[:, BRANCH_W:]
        v = v_ref[:, :BRANCH_W]
        cq = cq_ref[0]
        ck = ck_ref[0]
        for h in range(N_HEADS):
            sl = slice(h * HEAD_DIM, (h + 1) * HEAD_DIM)
            s = _dot_nt(q[:, sl].astype(BF), k[:, sl].astype(BF)) + cq[:, h:h + 1] - ck[h:h + 1, :]
            if mask is not None:
                s = jnp.where(mask, s, NEG)
            _online(s, _pv(v[:, sl].astype(BF)), m_scr.at[h], l_scr.at[h], acc_scr.at[:, sl])

    @pl.when(ki < qi)
    def _():
        process(None)

    @pl.when(ki == qi)
    def _():
        process(_causal(tq))

    @pl.when(ki == pl.num_programs(2) - 1)
    def _():
        for h in range(N_HEADS):
            sl = slice(h * HEAD_DIM, (h + 1) * HEAD_DIM)
            o_ref[:, sl] = acc_scr[:, sl] / l_scr[h][:, :1]


def _attn_scratch(tq, dv):
    return [pltpu.VMEM((N_HEADS, tq, LANES), F32), pltpu.VMEM((N_HEADS, tq, LANES), F32),
            pltpu.VMEM((tq, dv), F32)]


def _fox_prompt(z, cum, cum_t, B, T, tq):
    nq = T // tq
    zq = lambda ch: pl.BlockSpec((None, tq, CHUNK), lambda b, qi, ki: (ch, b * nq + qi, 0))
    zk = lambda ch: pl.BlockSpec((None, tq, CHUNK), lambda b, qi, ki: (ch, b * nq + jnp.minimum(ki, qi), 0))
    return pl.pallas_call(
        functools.partial(_fox_kernel, tq=tq), grid=(B, nq, nq),
        in_specs=[zq(CH_FQK), zk(CH_FQK), zk(CH_FV_BQ),
                  pl.BlockSpec((1, tq, N_HEADS), lambda b, qi, ki: (b, qi, 0)),
                  pl.BlockSpec((1, N_HEADS, tq), lambda b, qi, ki: (b, 0, jnp.minimum(ki, qi)))],
        out_specs=pl.BlockSpec((tq, BRANCH_W), lambda b, qi, ki: (b * nq + qi, 0)),
        out_shape=jax.ShapeDtypeStruct((B * T, BRANCH_W), F32),
        scratch_shapes=_attn_scratch(tq, BRANCH_W),
        compiler_params=_params(("arbitrary", "arbitrary", "arbitrary")), name="fox_prompt",
    )(z, z, z, cum, cum_t)


_MLA_SCALE = (MLA_NOPE + MLA_ROPE) ** -0.5


def _mla_kernel(ql_ref, qr_ref, ckv_ref, kr_ref, wuv_ref, o_ref, m_scr, l_scr, acc_scr, *, tq):
    qi, ki = pl.program_id(1), pl.program_id(2)

    @pl.when(ki == 0)
    def _():
        _init_stats(m_scr, l_scr, acc_scr)

    def process(mask):
        ckv = ckv_ref[...].astype(BF)
        kr = kr_ref[:, :MLA_ROPE].astype(BF)
        ql = ql_ref[...]
        qr = qr_ref[...]
        for h in range(N_HEADS):
            sl = slice(h * MLA_KV_RANK, (h + 1) * MLA_KV_RANK)
            s = (_dot_nt(ql[:, sl].astype(BF), ckv) +
                 _dot_nt(qr[:, h * MLA_ROPE:(h + 1) * MLA_ROPE].astype(BF), kr)) * _MLA_SCALE
            if mask is not None:
                s = jnp.where(mask, s, NEG)
            _online(s, _pv(ckv), m_scr.at[h], l_scr.at[h], acc_scr.at[:, sl])

    @pl.when(ki < qi)
    def _():
        process(None)

    @pl.when(ki == qi)
    def _():
        process(_causal(tq))

    @pl.when(ki == pl.num_programs(2) - 1)
    def _():
        for h in range(N_HEADS):
            sl = slice(h * MLA_KV_RANK, (h + 1) * MLA_KV_RANK)
            o_lat = (acc_scr[:, sl] / l_scr[h][:, :1]).astype(BF)
            o_ref[:, h * HEAD_DIM:(h + 1) * HEAD_DIM] = jnp.dot(o_lat, wuv_ref[h], preferred_element_type=F32)


def _mla_prompt(qlat, qrope, ckv, krlf, wuv, B, T, tq):
    nq = T // tq
    qs = lambda w: pl.BlockSpec((tq, w), lambda b, qi, ki: (b * nq + qi, 0))
    ks = lambda w: pl.BlockSpec((tq, w), lambda b, qi, ki: (b * nq + jnp.minimum(ki, qi), 0))
    return pl.pallas_call(
        functools.partial(_mla_kernel, tq=tq), grid=(B, nq, nq),
        in_specs=[qs(512), qs(128), ks(128), ks(128),
                  pl.BlockSpec(wuv.shape, lambda b, qi, ki: (0, 0, 0))],
        out_specs=pl.BlockSpec((tq, BRANCH_W), lambda b, qi, ki: (b * nq + qi, 0)),
        out_shape=jax.ShapeDtypeStruct((B * T, BRANCH_W), F32),
        scratch_shapes=_attn_scratch(tq, N_HEADS * MLA_KV_RANK),
        compiler_params=_params(("arbitrary", "arbitrary", "arbitrary")), name="mla_prompt",
    )(qlat, qrope, ckv, krlf, wuv)


def _relbias_kernel(tab_ref, qpos_ref, kpos_ref, o_ref):
    dist = jnp.maximum(qpos_ref[...] - kpos_ref[...], 0)
    tab = tab_ref[...]
    out = jnp.broadcast_to(tab[:, 0:1], dist.shape)
    for k in range(1, REL_BUCKETS):
        out = jnp.where(dist >= _REL_THR[k - 1], tab[:, k:k + 1], out)
    o_ref[...] = out


def _relbias(tab_rows, qpos, kpos):
    R, C = tab_rows.shape[0], kpos.shape[1]
    rt = min(R, 512)
    return pl.pallas_call(
        _relbias_kernel, grid=(R // rt,),
        in_specs=[pl.BlockSpec((rt, REL_BUCKETS), lambda i: (i, 0)),
                  pl.BlockSpec((rt, 1), lambda i: (i, 0)),
                  pl.BlockSpec((1, C), lambda i: (0, 0))],
        out_specs=pl.BlockSpec((rt, C), lambda i: (i, 0)),
        out_shape=jax.ShapeDtypeStruct((R, C), F32),
        compiler_params=_params(("arbitrary",)), name="rel_bias",
    )(tab_rows, qpos, kpos)


def _kmean_kernel(z_ref, o_ref):
    o_ref[0] = jnp.mean(z_ref[:, :BRANCH_W], axis=0, keepdims=True)


def _kmean_prompt(z, N):
    nb = N // MOBA_BLOCK
    return pl.pallas_call(
        _kmean_kernel, grid=(nb,),
        in_specs=[pl.BlockSpec((None, MOBA_BLOCK, CHUNK), lambda i: (CH_BKV, i, 0))],
        out_specs=pl.BlockSpec((1, 1, BRANCH_W), lambda i: (i, 0, 0)),
        out_shape=jax.ShapeDtypeStruct((nb, 1, BRANCH_W), F32),
        compiler_params=_params(("arbitrary",)), name="moba_kmean",
    )(z)


def _top_blocks(gate, n_valid, n_blocks):
    lane = lax.broadcasted_iota(jnp.int32, gate.shape, 1)
    rank = jnp.zeros(gate.shape, F32)
    for m in range(n_blocks):
        gm = gate[:, m:m + 1]
        ahead = jnp.where(gm > gate, 1.0, jnp.where(gm == gate, jnp.where(lane > m, 1.0, 0.0), 0.0))
        rank = rank + ahead * jnp.where(m < n_valid, 1.0, 0.0)
    return jnp.where(lane < n_valid, jnp.where(rank < MOBA_TOPK, 1.0, 0.0), 0.0)


def _moba_kernel(q_ref, kv_ref, km_ref, bias_ref, o_ref, m_scr, l_scr, acc_scr, sel_scr, *, nb):
    qi, ki = pl.program_id(1), pl.program_id(2)
    tq = MOBA_BLOCK

    @pl.when(ki == 0)
    def _():
        _init_stats(m_scr, l_scr, acc_scr)
        q = q_ref[:, BRANCH_W:]
        km = km_ref[0]
        for h in range(N_HEADS):
            sl = slice(h * HEAD_DIM, (h + 1) * HEAD_DIM)
            g = _dot_f32(q[:, sl], km[:, sl], _NT)
            if nb < LANES:
                g = jnp.concatenate([g, jnp.zeros((tq, LANES - nb), F32)], axis=1)
            sel_scr[h] = _top_blocks(g, qi, nb)

    def process(kind):
        q = q_ref[:, BRANCH_W:] * QK_SCALE
        k = kv_ref[:, :BRANCH_W]
        v = kv_ref[:, BRANCH_W:]
        lane = lax.broadcasted_iota(jnp.int32, (tq, LANES), 1)
        for h in range(N_HEADS):
            sl = slice(h * HEAD_DIM, (h + 1) * HEAD_DIM)
            s = _dot_nt(q[:, sl].astype(BF), k[:, sl].astype(BF))
            stats = (m_scr.at[h], l_scr.at[h], acc_scr.at[:, sl])
            pv = _pv(v[:, sl].astype(BF))
            if kind == "diag":
                s = jnp.where(_causal(tq), s + bias_ref[0, h], NEG)
                _online(s, pv, *stats)
                continue
            picked = jnp.sum(jnp.where(lane == ki, sel_scr[h], 0.0), axis=-1, keepdims=True) > 0.0
            if kind == "near":
                _online(s + bias_ref[0, h], pv, *stats, row_on=picked)
            else:
                _online(s, pv, *stats, row_on=picked, shift=bias_ref[0, h][0:1, 0:1])

    @pl.when(ki < qi - 1)
    def _():
        process("far")

    @pl.when(ki == qi - 1)
    def _():
        process("near")

    @pl.when(ki == qi)
    def _():
        process("diag")

    @pl.when(ki == pl.num_programs(2) - 1)
    def _():
        for h in range(N_HEADS):
            sl = slice(h * HEAD_DIM, (h + 1) * HEAD_DIM)
            o_ref[:, sl] = acc_scr[:, sl] / l_scr[h][:, :1]


def _moba_prompt(z, kmean, bias, B, T):
    tq = MOBA_BLOCK
    nq = T // tq
    assert nq <= LANES
    return pl.pallas_call(
        functools.partial(_moba_kernel, nb=nq), grid=(B, nq, nq),
        in_specs=[pl.BlockSpec((None, tq, CHUNK), lambda b, qi, ki: (CH_FV_BQ, b * nq + qi, 0)),
                  pl.BlockSpec((None, tq, CHUNK), lambda b, qi, ki: (CH_BKV, b * nq + jnp.minimum(ki, qi), 0)),
                  pl.BlockSpec((1, nq, BRANCH_W), lambda b, qi, ki: (b, 0, 0)),
                  pl.BlockSpec((1, N_HEADS, tq, tq),
                               lambda b, qi, ki: (jnp.clip(qi - ki, 0, 2), 0, 0, 0))],
        out_specs=pl.BlockSpec((tq, BRANCH_W), lambda b, qi, ki: (b * nq + qi, 0)),
        out_shape=jax.ShapeDtypeStruct((B * T, BRANCH_W), F32),
        scratch_shapes=_attn_scratch(tq, BRANCH_W) + [pltpu.VMEM((N_HEADS, tq, LANES), F32)],
        compiler_params=_params(("arbitrary", "arbitrary", "arbitrary")), name="moba_prompt",
    )(z, z, kmean, bias)


def _head_rows(x, width):
    return jnp.concatenate([x[:, h * width:(h + 1) * width] for h in range(N_HEADS)], axis=0)


def _blockdiag_rows(q):
    lane_head = lax.broadcasted_iota(jnp.int32, q.shape, 1) // HEAD_DIM
    return jnp.concatenate([jnp.where(lane_head == h, q, 0.0) for h in range(N_HEADS)], axis=0)


def _diag_heads(o_all, tq):
    lane_head = lax.broadcasted_iota(jnp.int32, (tq, o_all.shape[1]), 1) // HEAD_DIM
    out = jnp.zeros((tq, o_all.shape[1]), F32)
    for h in range(N_HEADS):
        out = out + jnp.where(lane_head == h, o_all[h * tq:(h + 1) * tq], 0.0)
    return out


def _rows_per_head(x, tq):
    return jnp.concatenate([jnp.broadcast_to(x[h:h + 1], (tq, x.shape[1])) for h in range(N_HEADS)], axis=0)


def _page_specs(block, l, pg, np_):
    zeros = (0,) * (len(block) - 2)
    return [pl.BlockSpec(block, lambda b, s, pt, p=p: (l, pt[b * np_ + s * pg + p]) + zeros) for p in range(pg)]


_KT_BLOCK = (None, None, N_HEADS, HEAD_DIM, PAGE)


def _kt(ref):
    return ref[...].reshape(BRANCH_W, PAGE).astype(BF)


def _new_rows_mask(R, tq):
    row_t = lax.broadcasted_iota(jnp.int32, (R, LANES), 0) % tq
    col = lax.broadcasted_iota(jnp.int32, (R, LANES), 1)
    return col <= row_t


def _tile_update(s_all, pv_fn, m_scr, l_scr, acc_scr):
    m_prev = m_scr[...][:, :1]
    m_new = jnp.maximum(m_prev, jnp.max(s_all, axis=-1, keepdims=True))
    alpha = jnp.exp(m_prev - m_new)
    p = jnp.exp(s_all - m_new)
    l_new = alpha * l_scr[...][:, :1] + jnp.sum(p, axis=-1, keepdims=True)
    l_scr[...] = jnp.broadcast_to(l_new, l_scr.shape)
    m_scr[...] = jnp.broadcast_to(m_new, m_scr.shape)
    acc_scr[...] = alpha * acc_scr[...] + pv_fn(p.astype(BF))


def _fox_dec_kernel(pt_ref, zq_ref, zv_ref, lfn_ref, *rest, pg, tq):
    k_refs, v_refs, lf_refs = rest[:pg], rest[pg:2 * pg], rest[2 * pg:3 * pg]
    o_ref = rest[3 * pg]
    qbd, m_scr, l_scr, acc_scr, run_scr, padk, padv = rest[3 * pg + 1:]
    s_idx = pl.program_id(1)
    n_steps = pl.num_programs(1)
    R = N_HEADS * tq
    G = pg * N_HEADS

    @pl.when(s_idx == 0)
    def _():
        _init_stats(m_scr, l_scr, acc_scr)
        qbd[...] = _blockdiag_rows(zq_ref[:, :BRANCH_W] * QK_SCALE).astype(BF)
        run_scr[...] = jnp.zeros(run_scr.shape, F32)

    def tile(with_new):
        upper = _upper_tri()
        lf = jnp.concatenate([lf_refs[p][...] for p in range(pg)], axis=0)
        within = _dot_exact01(lf, upper)
        tot = jnp.broadcast_to(within[:, LANES - 1:LANES], (G, LANES))
        r = lax.broadcasted_iota(jnp.int32, (G, G), 0)
        c = lax.broadcasted_iota(jnp.int32, (G, G), 1)
        same_head = (r % N_HEADS) == (c % N_HEADS)
        earlier = jnp.where(same_head, jnp.where(c // N_HEADS < r // N_HEADS, 1.0, 0.0), 0.0).astype(BF)
        run = run_scr[...]
        fk = run + _dot01_exact(earlier, tot) + within
        run_new = run + _dot01_exact(jnp.where(same_head, 1.0, 0.0).astype(BF), tot)
        run_scr[...] = run_new
        q = qbd[...]
        parts = [jnp.dot(q, _kt(k_refs[p]), preferred_element_type=F32)
                 - _rows_per_head(fk[p * N_HEADS:(p + 1) * N_HEADS], tq) for p in range(pg)]
        if with_new:
            lfn8 = jnp.concatenate([lfn_ref[...], jnp.zeros((8 - N_HEADS, LANES), F32)], axis=0)
            fkn = run_new[0:8] + _dot_exact01(lfn8, upper)
            s_new = _dot_nt(q, padk[...].astype(BF)) - _rows_per_head(fkn, tq)
            parts.append(jnp.where(_new_rows_mask(R, tq), s_new, NEG))

        def pv_fn(p):
            out = None
            for i in range(pg):
                t = _dot_nt(p[:, i * PAGE:(i + 1) * PAGE], _kt(v_refs[i]))
                out = t if out is None else out + t
            if with_new:
                out = out + jnp.dot(p[:, pg * PAGE:], padv[...].astype(BF), preferred_element_type=F32)
            return out

        _tile_update(jnp.concatenate(parts, axis=1), pv_fn, m_scr, l_scr, acc_scr)

    @pl.when(s_idx < n_steps - 1)
    def _():
        tile(False)

    @pl.when(s_idx == n_steps - 1)
    def _():
        padk[...] = jnp.zeros(padk.shape, F32)
        padv[...] = jnp.zeros(padv.shape, F32)
        padk[0:tq, :] = zq_ref[:, BRANCH_W:]
        padv[0:tq, :] = zv_ref[:, :BRANCH_W]
        tile(True)
        o_ref[...] = _diag_heads(acc_scr[...] / l_scr[...][:, :1], tq)


def _fox_decode(z, lf_new_t, pool_kt, pool_vt, pool_lft, pt_flat, l, B, tq, np_, pg):
    R = N_HEADS * tq
    grid_spec = pltpu.PrefetchScalarGridSpec(
        num_scalar_prefetch=1, grid=(B, np_ // pg),
        in_specs=[pl.BlockSpec((None, tq, CHUNK), lambda b, s, pt: (CH_FQK, b, 0)),
                  pl.BlockSpec((None, tq, CHUNK), lambda b, s, pt: (CH_FV_BQ, b, 0)),
                  pl.BlockSpec((None, N_HEADS, LANES), lambda b, s, pt: (b, 0, 0))]
        + _page_specs(_KT_BLOCK, l, pg, np_) + _page_specs(_KT_BLOCK, l, pg, np_)
        + _page_specs((None, None, N_HEADS, PAGE), l, pg, np_),
        out_specs=pl.BlockSpec((tq, BRANCH_W), lambda b, s, pt: (b, 0)),
        scratch_shapes=[pltpu.VMEM((R, BRANCH_W), BF), pltpu.VMEM((R, LANES), F32), pltpu.VMEM((R, LANES), F32),
                        pltpu.VMEM((R, BRANCH_W), F32), pltpu.VMEM((pg * N_HEADS, LANES), F32),
                        pltpu.VMEM((PAGE, BRANCH_W), F32), pltpu.VMEM((PAGE, BRANCH_W), F32)])
    return pl.pallas_call(
        functools.partial(_fox_dec_kernel, pg=pg, tq=tq), grid_spec=grid_spec,
        out_shape=jax.ShapeDtypeStruct((B * tq, BRANCH_W), F32),
        compiler_params=_params(("arbitrary", "arbitrary")), name="fox_decode",
    )(pt_flat, z, z, lf_new_t, *([pool_kt] * pg), *([pool_vt] * pg), *([pool_lft] * pg))


def _mla_dec_kernel(pt_ref, ql_ref, qr_ref, cn_ref, kn_ref, wuv_ref, *rest, pg, tq):
    c_refs, r_refs = rest[:pg], rest[pg:2 * pg]
    o_ref = rest[2 * pg]
    ql_scr, qr_scr, m_scr, l_scr, acc_scr, padc, padr = rest[2 * pg + 1:]
    s_idx = pl.program_id(1)
    n_steps = pl.num_programs(1)
    R = N_HEADS * tq

    @pl.when(s_idx == 0)
    def _():
        _init_stats(m_scr, l_scr, acc_scr)
        ql_scr[...] = _head_rows(ql_ref[...], MLA_KV_RANK).astype(BF)
        qr_scr[...] = _head_rows(qr_ref[...], MLA_ROPE).astype(BF)

    def tile(with_new):
        ql, qr = ql_scr[...], qr_scr[...]
        cs = [c_refs[p][...].astype(BF) for p in range(pg)]
        parts = [(_dot_nt(ql, cs[p]) + jnp.dot(qr, r_refs[p][...].astype(BF), preferred_element_type=F32))
                 * _MLA_SCALE for p in range(pg)]
        if with_new:
            s_new = (_dot_nt(ql, padc[...].astype(BF)) + _dot_nt(qr, padr[...].astype(BF))) * _MLA_SCALE
            parts.append(jnp.where(_new_rows_mask(R, tq), s_new, NEG))

        def pv_fn(p):
            out = None
            for i in range(pg):
                t = jnp.dot(p[:, i * PAGE:(i + 1) * PAGE], cs[i], preferred_element_type=F32)
                out = t if out is None else out + t
            if with_new:
                out = out + jnp.dot(p[:, pg * PAGE:], padc[...].astype(BF), preferred_element_type=F32)
            return out

        _tile_update(jnp.concatenate(parts, axis=1), pv_fn, m_scr, l_scr, acc_scr)

    @pl.when(s_idx < n_steps - 1)
    def _():
        tile(False)

    @pl.when(s_idx == n_steps - 1)
    def _():
        padc[...] = jnp.zeros(padc.shape, F32)
        padr[...] = jnp.zeros(padr.shape, F32)
        padc[0:tq, :] = cn_ref[...]
        padr[0:tq, :] = kn_ref[:, :MLA_ROPE]
        tile(True)
        o_lat = (acc_scr[...] / l_scr[...][:, :1]).astype(BF)
        for h in range(N_HEADS):
            o_ref[:, h * HEAD_DIM:(h + 1) * HEAD_DIM] = jnp.dot(
                o_lat[h * tq:(h + 1) * tq], wuv_ref[h], preferred_element_type=F32)


def _mla_decode(qlat, qrope, ckv, krlf, wuv, pool_c, pool_rt, pt_flat, l, B, tq, np_, pg):
    R = N_HEADS * tq
    row = lambda w: pl.BlockSpec((tq, w), lambda b, s, pt: (b, 0))
    grid_spec = pltpu.PrefetchScalarGridSpec(
        num_scalar_prefetch=1, grid=(B, np_ // pg),
        in_specs=[row(512), row(128), row(128), row(128),
                  pl.BlockSpec(wuv.shape, lambda b, s, pt: (0, 0, 0))]
        + _page_specs((None, None, PAGE, MLA_KV_RANK), l, pg, np_)
        + _page_specs((None, None, MLA_ROPE, PAGE), l, pg, np_),
        out_specs=pl.BlockSpec((tq, BRANCH_W), lambda b, s, pt: (b, 0)),
        scratch_shapes=[pltpu.VMEM((R, MLA_KV_RANK), BF), pltpu.VMEM((R, MLA_ROPE), BF),
                        pltpu.VMEM((R, LANES), F32), pltpu.VMEM((R, LANES), F32),
                        pltpu.VMEM((R, MLA_KV_RANK), F32),
                        pltpu.VMEM((PAGE, MLA_KV_RANK), F32), pltpu.VMEM((PAGE, MLA_ROPE), F32)])
    return pl.pallas_call(
        functools.partial(_mla_dec_kernel, pg=pg, tq=tq), grid_spec=grid_spec,
        out_shape=jax.ShapeDtypeStruct((B * tq, BRANCH_W), F32),
        compiler_params=_params(("arbitrary", "arbitrary")), name="mla_decode",
    )(pt_flat, qlat, qrope, ckv, krlf, wuv, *([pool_c] * pg), *([pool_rt] * pg))


def _moba_dec_kernel(pt_ref, zq_ref, zkv_ref, bias_ref, *rest, pg, tq, nb):
    k_refs, v_refs = rest[:pg], rest[pg:2 * pg]
    o_ref = rest[2 * pg]
    qbd, qf_scr, km_scr, mb_scr, lb_scr, ob_scr, padk, padv = rest[2 * pg + 1:]
    s_idx = pl.program_id(1)
    R = N_HEADS * tq

    @pl.when(s_idx == 0)
    def _():
        q = _blockdiag_rows(zq_ref[:, BRANCH_W:] * QK_SCALE)
        qf_scr[...] = q
        qbd[...] = q.astype(BF)
        km_scr[...] = jnp.zeros(km_scr.shape, F32)

    def block_stats(s_parts, pv_fn):
        m = s_parts[0].max(axis=-1, keepdims=True)
        for s in s_parts[1:]:
            m = jnp.maximum(m, s.max(axis=-1, keepdims=True))
        ps = [jnp.exp(s - m) for s in s_parts]
        lsum = ps[0].sum(axis=-1, keepdims=True)
        for p in ps[1:]:
            lsum = lsum + p.sum(axis=-1, keepdims=True)
        return m, lsum, pv_fn([p.astype(BF) for p in ps])

    b_far = bias_ref[2 * R:3 * R, :]
    b_last = bias_ref[R:2 * R, :]
    lane = lax.broadcasted_iota(jnp.int32, (BRANCH_W, LANES), 1)
    for j in range(pg // 2):
        n = s_idx * (pg // 2) + j
        kta, ktb = k_refs[2 * j][...].reshape(BRANCH_W, PAGE), k_refs[2 * j + 1][...].reshape(BRANCH_W, PAGE)
        kmean = jnp.sum(kta + ktb, axis=-1, keepdims=True) * (1.0 / MOBA_BLOCK)
        km_scr[...] = jnp.where(lane == n, kmean, km_scr[...])
        bias = jnp.where(n == nb - 1, b_last, b_far)
        q = qbd[...]
        s_a = jnp.dot(q, kta.astype(BF), preferred_element_type=F32) + bias[:, :PAGE]
        s_b = jnp.dot(q, ktb.astype(BF), preferred_element_type=F32) + bias[:, PAGE:]
        m, lsum, o = block_stats(
            [s_a, s_b], lambda ps, j=j: _dot_nt(ps[0], _kt(v_refs[2 * j])) + _dot_nt(ps[1], _kt(v_refs[2 * j + 1])))
        mb_scr[n] = jnp.broadcast_to(m, (R, LANES))
        lb_scr[n] = jnp.broadcast_to(lsum, (R, LANES))
        ob_scr[n] = o

    @pl.when(s_idx == pl.num_programs(1) - 1)
    def _():
        padk[...] = jnp.zeros(padk.shape, F32)
        padv[...] = jnp.zeros(padv.shape, F32)
        padk[0:tq, :] = zkv_ref[:, :BRANCH_W]
        padv[0:tq, :] = zkv_ref[:, BRANCH_W:]
        s_own = jnp.where(_new_rows_mask(R, tq),
                          _dot_nt(qbd[...], padk[...].astype(BF)) + bias_ref[0:R, :PAGE], NEG)
        m_tot, l_tot, o_tot = block_stats(
            [s_own], lambda ps: jnp.dot(ps[0], padv[...].astype(BF), preferred_element_type=F32))
        g = _dot_f32(qf_scr[...], km_scr[...], _NN)
        sel = _top_blocks(g, nb, nb)
        for n in range(nb):
            on = sel[:, n:n + 1] > 0.0
            m_n = jnp.where(on, mb_scr[n][:, :1], NEG)
            m_new = jnp.maximum(m_tot, m_n)
            a_old = jnp.exp(m_tot - m_new)
            a_n = jnp.where(on, jnp.exp(m_n - m_new), 0.0)
            l_tot = a_old * l_tot + a_n * lb_scr[n][:, :1]
            o_tot = a_old * o_tot + a_n * ob_scr[n]
            m_tot = m_new
        o_ref[...] = _diag_heads(o_tot / l_tot, tq)


def _moba_decode(z, bias, pool_kt, pool_vt, pt_flat, l, B, tq, np_, pg):
    R = N_HEADS * tq
    nb = np_ * PAGE // MOBA_BLOCK
    assert pg % 2 == 0 and nb <= LANES
    grid_spec = pltpu.PrefetchScalarGridSpec(
        num_scalar_prefetch=1, grid=(B, np_ // pg),
        in_specs=[pl.BlockSpec((None, tq, CHUNK), lambda b, s, pt: (CH_FV_BQ, b, 0)),
                  pl.BlockSpec((None, tq, CHUNK), lambda b, s, pt: (CH_BKV, b, 0)),
                  pl.BlockSpec(bias.shape, lambda b, s, pt: (0, 0))]
        + _page_specs(_KT_BLOCK, l, pg, np_) + _page_specs(_KT_BLOCK, l, pg, np_),
        out_specs=pl.BlockSpec((tq, BRANCH_W), lambda b, s, pt: (b, 0)),
        scratch_shapes=[pltpu.VMEM((R, BRANCH_W), BF), pltpu.VMEM((R, BRANCH_W), F32),
                        pltpu.VMEM((BRANCH_W, LANES), F32),
                        pltpu.VMEM((nb, R, LANES), F32), pltpu.VMEM((nb, R, LANES), F32),
                        pltpu.VMEM((nb, R, BRANCH_W), F32),
                        pltpu.VMEM((PAGE, BRANCH_W), F32), pltpu.VMEM((PAGE, BRANCH_W), F32)])
    return pl.pallas_call(
        functools.partial(_moba_dec_kernel, pg=pg, tq=tq, nb=nb), grid_spec=grid_spec,
        out_shape=jax.ShapeDtypeStruct((B * tq, BRANCH_W), F32),
        compiler_params=_params(("arbitrary", "arbitrary")), name="moba_decode",
    )(pt_flat, z, z, bias, *([pool_kt] * pg), *([pool_vt] * pg))


_HALO = 32


def _conv_kernel(z_ref, hist_ref, w_ref, b_ref, g_ref, beta_ref, y_ref, buf_ref, ext, *, tt):
    ti = pl.program_id(1)

    @pl.when(ti == 0)
    def _():
        ext[0:_HALO, :] = hist_ref[0]

    @pl.when(ti > 0)
    def _():
        ext[0:_HALO, :] = ext[tt:tt + _HALO, :]

    z = z_ref[...]
    ext[_HALO:_HALO + tt, :] = z[:, :BRANCH_W] * jax.nn.sigmoid(z[:, BRANCH_W:])
    first = _HALO - (CONV_W - 1)
    y = jnp.zeros((tt, BRANCH_W), F32)
    for j in range(CONV_W):
        y = y + w_ref[j:j + 1, :] * ext[first + j:first + j + tt, :]
    y = y + b_ref[...]
    mu = jnp.mean(y, axis=-1, keepdims=True)
    var = jnp.mean(jnp.square(y - mu), axis=-1, keepdims=True)
    yn = (y - mu) * lax.rsqrt(var + EPS) * g_ref[...] + beta_ref[...]
    y_ref[...] = yn * jax.nn.sigmoid(yn)

    @pl.when(ti == pl.num_programs(1) - 1)
    def _():
        buf_ref[0] = ext[tt + first:tt + _HALO, :]


def _conv(z, hist32, w_dw, b_dw, ln_g, ln_b, B, T, tt):
    nt = T // tt
    vec = lambda: pl.BlockSpec((1, BRANCH_W), lambda b, t: (0, 0))
    return pl.pallas_call(
        functools.partial(_conv_kernel, tt=tt), grid=(B, nt),
        in_specs=[pl.BlockSpec((None, tt, CHUNK), lambda b, t: (CH_CONV, b * nt + t, 0)),
                  pl.BlockSpec((1, _HALO, BRANCH_W), lambda b, t: (b, 0, 0)),
                  pl.BlockSpec((CONV_W, BRANCH_W), lambda b, t: (0, 0)), vec(), vec(), vec()],
        out_specs=[pl.BlockSpec((tt, BRANCH_W), lambda b, t: (b * nt + t, 0)),
                   pl.BlockSpec((1, CONV_W - 1, BRANCH_W), lambda b, t: (b, 0, 0))],
        out_shape=[jax.ShapeDtypeStruct((B * T, BRANCH_W), F32),
                   jax.ShapeDtypeStruct((B, CONV_W - 1, BRANCH_W), F32)],
        scratch_shapes=[pltpu.VMEM((tt + _HALO, BRANCH_W), F32)],
        compiler_params=_params(("arbitrary", "arbitrary")), name="conv_module",
    )(z, hist32, w_dw, b_dw.reshape(1, -1), ln_g.reshape(1, -1), ln_b.reshape(1, -1))


def _merge_kernel(x_ref, g1_ref, gate_ref, of_ref, om_ref, ob_ref, oc_ref, wb_ref, wo_ref, o_ref):
    merged = None
    for n, br in enumerate((of_ref, om_ref, ob_ref, oc_ref)):
        proj = jnp.dot(br[...].astype(BF), wb_ref[n], preferred_element_type=F32)
        sig = jnp.concatenate([gate_ref[2 * n], gate_ref[2 * n + 1]], axis=-1)
        merged = sig * proj if merged is None else merged + sig * proj
    out = jnp.dot(merged.astype(BF), wo_ref[...], preferred_element_type=F32)
    o_ref[...] = x_ref[...] + g1_ref[...] * out.reshape(x_ref.shape)


def _merge(tok, x, mod, l, z, o_fox, o_mla, o_moba, y_conv, w_branch, w_out):
    return pl.pallas_call(
        _merge_kernel, grid=(tok.n_tiles,),
        in_specs=[tok.x_spec(), tok.mod_spec(l, 2), tok.z_spec(0, 8)]
        + [tok.flat_spec(BRANCH_W)] * 4
        + [pl.BlockSpec(w_branch.shape, lambda i: (0, 0, 0)), pl.BlockSpec(w_out.shape, lambda i: (0, 0))],
        out_specs=tok.x_spec(),
        out_shape=jax.ShapeDtypeStruct(x.shape, F32),
        compiler_params=_params(("arbitrary",)), name="merge",
    )(x, mod, z, o_fox, o_mla, o_moba, y_conv, w_branch, w_out)


def _ffn_kernel(x_ref, sh_ref, sc_ref, g2_ref, gn_ref, wr_ref, br_ref, wg_ref, wu_ref, wd_ref, gout_ref,
                o_ref, h_scr, comb_scr, acc_e, acc_o, *, n_exp, final):
    e, f = pl.program_id(1), pl.program_id(2)
    nf = pl.num_programs(2)
    tm = h_scr.shape[0]

    @pl.when((e == 0) & (f == 0))
    def _():
        h = _modulated_norm(x_ref[...], gn_ref[...], sc_ref[...], sh_ref[...]).reshape(h_scr.shape)
        h_scr[...] = h.astype(BF)
        acc_o[...] = jnp.zeros(acc_o.shape, F32)
        if n_exp > 1:
            lane = lax.broadcasted_iota(jnp.int32, (tm, LANES), 1)
            logits = _dot_f32(h, wr_ref[...], _NT) + br_ref[...]
            logits = jnp.where(lane < n_exp, logits, NEG)
            m1 = jnp.max(logits, axis=-1, keepdims=True)
            i1 = jnp.min(jnp.where(logits == m1, lane, LANES), axis=-1, keepdims=True)
            rest = jnp.where(lane == i1, NEG, logits)
            m2 = jnp.max(rest, axis=-1, keepdims=True)
            i2 = jnp.min(jnp.where(rest == m2, lane, LANES), axis=-1, keepdims=True)
            e2 = jnp.exp(m2 - m1)
            w1 = 1.0 / (1.0 + e2)
            w2 = e2 / (1.0 + e2)
            comb_scr[...] = jnp.where(lane == i1, w1, 0.0) + jnp.where(lane == i2, w2, 0.0)

    @pl.when(f == 0)
    def _():
        acc_e[...] = jnp.zeros(acc_e.shape, F32)

    h = h_scr[...]
    a = jnp.dot(h, wg_ref[0], preferred_element_type=F32)
    u = jnp.dot(h, wu_ref[0], preferred_element_type=F32)
    act = (a * jax.nn.sigmoid(a) * u).astype(BF)
    acc_e[...] += jnp.dot(act, wd_ref[0], preferred_element_type=F32)

    @pl.when(f == nf - 1)
    def _():
        if n_exp > 1:
            lane = lax.broadcasted_iota(jnp.int32, (tm, LANES), 1)
            w = jnp.sum(jnp.where(lane == e, comb_scr[...], 0.0), axis=-1, keepdims=True)
            acc_o[...] += w * acc_e[...]
        else:
            acc_o[...] += acc_e[...]

    @pl.when((e == n_exp - 1) & (f == nf - 1))
    def _():
        xn = x_ref[...] + g2_ref[...] * acc_o[...].reshape(x_ref.shape)
        if final:
            ms = jnp.mean(xn * xn, axis=-1, keepdims=True)
            xn = xn * lax.rsqrt(ms + EPS) * gout_ref[...]
        o_ref[...] = xn


def _ffn(tok, x, mod, l, g_norm, w_router_t, b_router, wg, wu, wd, g_out, tf, final):
    n_exp, _, F = wg.shape
    nf = F // tf
    assert F % tf == 0 and tf % LANES == 0
    return pl.pallas_call(
        functools.partial(_ffn_kernel, n_exp=n_exp, final=final), grid=(tok.n_tiles, n_exp, nf),
        in_specs=[tok.x_spec(), tok.mod_spec(l, 3), tok.mod_spec(l, 4), tok.mod_spec(l, 5),
                  pl.BlockSpec((1, D_MODEL), lambda i, e, f: (0, 0)),
                  pl.BlockSpec(w_router_t.shape, lambda i, e, f: (0, 0)),
                  pl.BlockSpec((1, LANES), lambda i, e, f: (0, 0)),
                  pl.BlockSpec((1, D_MODEL, tf), lambda i, e, f: (e, 0, f)),
                  pl.BlockSpec((1, D_MODEL, tf), lambda i, e, f: (e, 0, f)),
                  pl.BlockSpec((1, tf, D_MODEL), lambda i, e, f: (e, f, 0)),
                  pl.BlockSpec((1, D_MODEL), lambda i, e, f: (0, 0))],
        out_specs=tok.x_spec(),
        out_shape=jax.ShapeDtypeStruct(x.shape, F32),
        scratch_shapes=[pltpu.VMEM((tok.tm, D_MODEL), BF), pltpu.VMEM((tok.tm, LANES), F32),
                        pltpu.VMEM((tok.tm, D_MODEL), F32), pltpu.VMEM((tok.tm, D_MODEL), F32)],
        compiler_params=_params(("arbitrary", "arbitrary", "arbitrary")), name="ffn",
    )(x, mod, mod, mod, g_norm.reshape(1, D_MODEL), w_router_t, b_router, wg, wu, wd, g_out.reshape(1, D_MODEL))


def _largest_tile(n, cap):
    t = min(n, cap)
    while n % t:
        t //= 2
    return t


def _ffn_tile(F):
    for tf in (512, 1408, 896, 256, 128):
        if F % tf == 0:
            return tf
    raise ValueError(F)


def kernel(x_prompt, x_sample, cache_fox_k, cache_fox_v, cache_fox_logf, cache_mla_ckv, cache_mla_krope,
           cache_moba_k, cache_moba_v, state_conv, page_table, c_prompt, c_sample,
           w_ada, b_ada, norm_mix, norm_ffn, w_in, b_fox_f, g_mla_q, g_mla_kv, w_mla_uq, w_mla_ukv,
           w_dw, b_dw, conv_ln_g, conv_ln_b, w_branch, w_out, rel_bias,
           ffn_w_gate, ffn_w_up, ffn_w_down, moe_w_router, moe_b_router, moe_w_gate, moe_w_up, moe_w_down,
           norm_out):
    Bp, Tp, D = x_prompt.shape
    Bs, Ts, _ = x_sample.shape
    L = w_ada.shape[0]
    np_ = page_table.shape[1]
    past_len = np_ * PAGE
    assert D == D_MODEL and Tp % MOBA_BLOCK == 0 and past_len % MOBA_BLOCK == 0 and Ts == 8
    pg = _largest_tile(np_, 16)

    n_c = Bp + Bs
    mc = -(-n_c // 8) * 8
    c_all = jnp.concatenate([c_prompt, c_sample, jnp.zeros((mc - n_c, D), F32)], axis=0)
    mod_all = _ada(c_all, w_ada, b_ada)
    mod_p = mod_all[:, :, :Bp, None, :]
    mod_s = mod_all[:, :, Bp:Bp + Bs, None, :]

    tok_p = _Tok(Bp, Tp, 1, _largest_tile(Tp, 512))
    tok_p_in = _Tok(Bp, Tp, 1, _largest_tile(Tp, 1024))
    tok_s = _Tok(Bs, Ts, _largest_tile(Bs, 64), Ts)
    tq_p = _largest_tile(Tp, 512)

    rope_p = _rope_tables(jnp.arange(Tp), 1)
    rope_s = _rope_tables(past_len + jnp.arange(Ts), tok_s.bb)

    tab = rel_bias.T.astype(F32)
    blk = MOBA_BLOCK
    tab_rows_p = jnp.tile(jnp.repeat(tab, blk, axis=0), (3, 1))
    qpos_p = (jnp.arange(3)[:, None, None] * blk + jnp.arange(blk)[None, None, :]
              + jnp.zeros((1, N_HEADS, 1), jnp.int32)).reshape(-1, 1).astype(jnp.int32)
    bias_p = _relbias(tab_rows_p, qpos_p, jnp.arange(blk, dtype=jnp.int32)[None, :])
    bias_p = bias_p.reshape(3, N_HEADS, blk, blk)
    tab_rows_s = jnp.tile(jnp.repeat(tab, Ts, axis=0), (3, 1))
    qpos_s = (jnp.arange(3)[:, None, None] * blk + jnp.arange(Ts)[None, None, :]
              + jnp.zeros((1, N_HEADS, 1), jnp.int32)).reshape(-1, 1).astype(jnp.int32)
    bias_s = _relbias(tab_rows_s, qpos_s, jnp.arange(blk, dtype=jnp.int32)[None, :])

    pt_flat = page_table.reshape(-1).astype(jnp.int32)
    page_t = lambda a: jnp.transpose(a, (0, 1, 3, 4, 2))
    pk_fox, pv_fox = page_t(cache_fox_k), page_t(cache_fox_v)
    pk_moba, pv_moba = page_t(cache_moba_k), page_t(cache_moba_v)
    p_lft = jnp.swapaxes(cache_fox_logf, 2, 3)
    p_krt = jnp.swapaxes(cache_mla_krope, 2, 3)
    hist_p = jnp.zeros((Bp, _HALO, BRANCH_W), F32)
    hist_s = jnp.pad(state_conv, ((0, 0), (0, 0), (_HALO - (CONV_W - 1), 0), (0, 0)))

    xp, xs = x_prompt, x_sample
    rows_p, rows_s = [], []
    for l in range(L):
        wt_perm = _perm_w_in_t(w_in[l])
        w_uq = w_mla_uq[l]
        wn = w_uq[:, :, :MLA_NOPE].reshape(MLA_Q_RANK, -1).astype(BF)
        wr = w_uq[:, :, MLA_NOPE:].reshape(MLA_Q_RANK, -1).astype(BF)
        wuk = jnp.transpose(w_mla_ukv[l][:, :, :MLA_NOPE], (1, 2, 0)).astype(BF)
        wuv = jnp.transpose(w_mla_ukv[l][:, :, MLA_NOPE:], (1, 0, 2)).astype(BF)
        bf128 = jnp.zeros((1, LANES), F32).at[0, MLA_ROPE:MLA_ROPE + N_HEADS].set(b_fox_f[l])
        gq, gkv = g_mla_q[l].reshape(1, -1), g_mla_kv[l].reshape(1, -1)
        wb, wo = w_branch[l].astype(BF), w_out[l].astype(BF)
        i = l // 2
        if l % 2 == 0:
            wg, wu, wd = ffn_w_gate[i][None].astype(BF), ffn_w_up[i][None].astype(BF), ffn_w_down[i][None].astype(BF)
            w_rt = jnp.zeros((LANES, D), F32)
            b_rt = jnp.zeros((1, LANES), F32)
        else:
            wg, wu, wd = moe_w_gate[i].astype(BF), moe_w_up[i].astype(BF), moe_w_down[i].astype(BF)
            n_e = wg.shape[0]
            w_rt = jnp.zeros((LANES, D), F32).at[:n_e].set(moe_w_router[i].T)
            b_rt = jnp.zeros((1, LANES), F32).at[0, :n_e].set(moe_b_router[i])
        tf = _ffn_tile(wg.shape[2])
        final = l == L - 1

        z = _inproj(tok_p_in, xp, mod_p, l, norm_mix[l], wt_perm)
        qlat, qrope, ckv, krlf = _prep(tok_p, z, gq, gkv, wn, wr, wuk, bf128, rope_p)
        logf = krlf[:, MLA_ROPE:MLA_ROPE + N_HEADS].reshape(Bp, Tp, N_HEADS)
        cum_t = _cumsum_time(jnp.swapaxes(logf, 1, 2).reshape(Bp * N_HEADS, Tp // LANES, LANES))
        cum_t = cum_t.reshape(Bp, N_HEADS, Tp)
        o_fox = _fox_prompt(z, jnp.swapaxes(cum_t, 1, 2), cum_t, Bp, Tp, tq_p)
        o_mla = _mla_prompt(qlat, qrope, ckv, krlf, wuv, Bp, Tp, tq_p)
        kmean = _kmean_prompt(z, Bp * Tp).reshape(Bp, Tp // blk, BRANCH_W)
        o_moba = _moba_prompt(z, kmean, bias_p, Bp, Tp)
        y_conv, buf_p = _conv(z, hist_p, w_dw[l], b_dw[l], conv_ln_g[l], conv_ln_b[l], Bp, Tp, tok_p.tt)
        xp = _merge(tok_p, xp, mod_p, l, z, o_fox, o_mla, o_moba, y_conv, wb, wo)
        xp = _ffn(tok_p, xp, mod_p, l, norm_ffn[l], w_rt, b_rt, wg, wu, wd, norm_out, tf, final)
        heads = lambda a: a.reshape(Bp, Tp, N_HEADS, HEAD_DIM)
        rows_p.append((heads(z[CH_FQK, :, BRANCH_W:]), heads(z[CH_FV_BQ, :, :BRANCH_W]), logf,
                       ckv.reshape(Bp, Tp, MLA_KV_RANK), krlf[:, :MLA_ROPE].reshape(Bp, Tp, MLA_ROPE),
                       heads(z[CH_BKV, :, :BRANCH_W]), heads(z[CH_BKV, :, BRANCH_W:]), buf_p))

        z = _inproj(tok_s, xs, mod_s, l, norm_mix[l], wt_perm)
        qlat, qrope, ckv, krlf = _prep(tok_s, z, gq, gkv, wn, wr, wuk, bf128, rope_s)
        logf = krlf[:, MLA_ROPE:MLA_ROPE + N_HEADS].reshape(Bs, Ts, N_HEADS)
        lf_new_t = jnp.pad(jnp.swapaxes(logf, 1, 2), ((0, 0), (0, 0), (0, LANES - Ts)))
        o_fox = _fox_decode(z, lf_new_t, pk_fox, pv_fox, p_lft, pt_flat, l, Bs, Ts, np_, pg)
        o_mla = _mla_decode(qlat, qrope, ckv, krlf, wuv, cache_mla_ckv, p_krt, pt_flat, l, Bs, Ts, np_, pg)
        o_moba = _moba_decode(z, bias_s, pk_moba, pv_moba, pt_flat, l, Bs, Ts, np_, pg)
        y_conv, buf_s = _conv(z, hist_s[l], w_dw[l], b_dw[l], conv_ln_g[l], conv_ln_b[l], Bs, Ts, Ts)
        xs = _merge(tok_s, xs, mod_s, l, z, o_fox, o_mla, o_moba, y_conv, wb, wo)
        xs = _ffn(tok_s, xs, mod_s, l, norm_ffn[l], w_rt, b_rt, wg, wu, wd, norm_out, tf, final)
        heads = lambda a: a.reshape(Bs, Ts, N_HEADS, HEAD_DIM)
        rows_s.append((heads(z[CH_FQK, :, BRANCH_W:]), heads(z[CH_FV_BQ, :, :BRANCH_W]), logf,
                       ckv.reshape(Bs, Ts, MLA_KV_RANK), krlf[:, :MLA_ROPE].reshape(Bs, Ts, MLA_ROPE),
                       heads(z[CH_BKV, :, :BRANCH_W]), heads(z[CH_BKV, :, BRANCH_W:]), buf_s))

    stack = lambda rows: [jnp.stack([r[j] for r in rows]) for j in range(8)]
    return (xp, xs, *stack(rows_p), *stack(rows_s))
```

```python
import functools
import math

import numpy as np
import jax
import jax.numpy as jnp
from jax import lax
from jax.experimental import pallas as pl
from jax.experimental.pallas import tpu as pltpu

F32 = jnp.float32
BF = jnp.bfloat16

D_MODEL = 1024
N_HEADS = 4
HEAD_DIM = 64
BRANCH_W = 256
MLA_Q_RANK = 256
MLA_KV_RANK = 128
MLA_NOPE = 64
MLA_ROPE = 32
CONV_W = 31
MOBA_BLOCK = 256
MOBA_TOPK = 3
REL_BUCKETS = 32
REL_MAX_DIST = 128
ROPE_THETA = 10000.0
EPS = 1e-6
PAGE = 128
IN_SIZES = (256, 256, 256, 4, 256, 128, 32, 256, 256, 256, 256, 256, 4096)

LANES = 128
CHUNK = 512
N_CHUNKS = 13
CH_FQK, CH_FV_BQ, CH_BKV, CH_MLA, CH_CONV = 8, 9, 10, 11, 12
NEG = -1e30
VMEM_LIMIT = 56 * 2 ** 20
QK_SCALE = HEAD_DIM ** -0.5


def _params(sem):
    return pltpu.CompilerParams(dimension_semantics=sem, vmem_limit_bytes=VMEM_LIMIT)


def _split3(x):
    x1 = x.astype(BF)
    r1 = x - x1.astype(F32)
    x2 = r1.astype(BF)
    x3 = (r1 - x2.astype(F32)).astype(BF)
    return x1, x2, x3


def _dot_exact01(x, m01):
    acc = None
    for piece in _split3(x):
        t = jnp.dot(piece, m01, preferred_element_type=F32)
        acc = t if acc is None else acc + t
    return acc


def _dot01_exact(m01, x):
    acc = None
    for piece in _split3(x):
        t = jnp.dot(m01, piece, preferred_element_type=F32)
        acc = t if acc is None else acc + t
    return acc


def _dot_f32(a, b, dn):
    a1, a2, a3 = _split3(a)
    b1, b2, b3 = _split3(b)
    acc = None
    for x, y in ((a1, b1), (a1, b2), (a2, b1), (a2, b2), (a1, b3), (a3, b1)):
        t = lax.dot_general(x, y, dn, preferred_element_type=F32)
        acc = t if acc is None else acc + t
    return acc


_NT = (((1,), (1,)), ((), ()))
_NN = (((1,), (0,)), ((), ()))


def _dot_nt(a, b):
    return lax.dot_general(a, b, _NT, preferred_element_type=F32)


def _log_sigmoid(x):
    return jnp.minimum(x, 0.0) - jnp.log1p(jnp.exp(-jnp.abs(x)))


def _rel_thresholds():
    max_exact = REL_BUCKETS // 2
    n = np.arange(0, 4 * REL_MAX_DIST)
    nf = np.maximum(n, 1).astype(np.float64)
    large = max_exact + (np.log(nf / max_exact) / math.log(REL_MAX_DIST / max_exact)
                         * (REL_BUCKETS - max_exact)).astype(np.int64)
    bucket = np.where(n < max_exact, n, np.minimum(large, REL_BUCKETS - 1))
    return [int(np.argmax(bucket >= k)) for k in range(1, REL_BUCKETS)]


_REL_THR = _rel_thresholds()


class _Tok:
    def __init__(self, B, T, bb, tt):
        assert B % bb == 0 and T % tt == 0 and (bb == 1 or tt == T) and tt % 8 == 0
        self.B, self.T, self.bb, self.tt = B, T, bb, tt
        self.nt = T // tt
        self.n_tiles = (B // bb) * self.nt
        self.tm = bb * tt
        self.N = B * T

    def x_spec(self):
        nt = self.nt
        return pl.BlockSpec((self.bb, self.tt, D_MODEL), lambda i, *_: (i // nt, i % nt, 0))

    def mod_spec(self, l, k):
        nt = self.nt
        return pl.BlockSpec((None, None, self.bb, 1, D_MODEL), lambda i, *_: (l, k, i // nt, 0, 0))

    def flat_spec(self, width):
        return pl.BlockSpec((self.tm, width), lambda i, *_: (i, 0))

    def z_spec(self, chunk, n=None):
        if n is None:
            return pl.BlockSpec((None, self.tm, CHUNK), lambda i, *_: (chunk, i, 0))
        return pl.BlockSpec((n, self.tm, CHUNK), lambda i, *_: (chunk // n, i, 0))


def _ada_kernel(c_ref, w_ref, b_ref, o_ref):
    c = c_ref[...]
    s = (c * jax.nn.sigmoid(c)).astype(BF)
    o_ref[...] = jnp.dot(s, w_ref[0].astype(BF), preferred_element_type=F32) + b_ref[0]


def _ada(c_all, w_ada, b_ada):
    L, D, D6 = w_ada.shape
    Mc = c_all.shape[0]
    tn = CHUNK
    per = D // tn
    return pl.pallas_call(
        _ada_kernel, grid=(L, D6 // tn),
        in_specs=[pl.BlockSpec((Mc, D), lambda l, j: (0, 0)),
                  pl.BlockSpec((1, D, tn), lambda l, j: (l, 0, j)),
                  pl.BlockSpec((1, 1, tn), lambda l, j: (l, 0, j))],
        out_specs=pl.BlockSpec((None, None, Mc, tn), lambda l, j: (l, j // per, 0, j % per)),
        out_shape=jax.ShapeDtypeStruct((L, D6 // D, Mc, D), F32),
        compiler_params=_params(("arbitrary", "arbitrary")), name="ada",
    )(c_all, w_ada, b_ada.reshape(L, 1, D6))


def _modulated_norm(x, g, sc, sh):
    ms = jnp.mean(x * x, axis=-1, keepdims=True)
    y = x * lax.rsqrt(ms + EPS) * g
    return y * (1.0 + sc) + sh


def _inproj_kernel(x_ref, sh_ref, sc_ref, g_ref, wt_ref, z_ref, h_scr):
    j = pl.program_id(1)

    @pl.when(j == 0)
    def _():
        h = _modulated_norm(x_ref[...], g_ref[...], sc_ref[...], sh_ref[...])
        h_scr[...] = h.reshape(h_scr.shape).astype(BF)

    acc = _dot_nt(h_scr[...], wt_ref[...])

    @pl.when(j < 8)
    def _():
        z_ref[...] = jax.nn.sigmoid(acc)

    @pl.when(j >= 8)
    def _():
        z_ref[...] = acc


def _inproj(tok, x, mod, l, g_norm, wt_perm):
    return pl.pallas_call(
        _inproj_kernel, grid=(tok.n_tiles, N_CHUNKS),
        in_specs=[tok.x_spec(), tok.mod_spec(l, 0), tok.mod_spec(l, 1),
                  pl.BlockSpec((1, D_MODEL), lambda i, j: (0, 0)),
                  pl.BlockSpec((CHUNK, D_MODEL), lambda i, j: (j, 0))],
        out_specs=pl.BlockSpec((None, tok.tm, CHUNK), lambda i, j: (j, i, 0)),
        out_shape=jax.ShapeDtypeStruct((N_CHUNKS, tok.N, CHUNK), F32),
        scratch_shapes=[pltpu.VMEM((tok.tm, D_MODEL), BF)],
        compiler_params=_params(("arbitrary", "arbitrary")), name="inproj",
    )(x, mod, mod, g_norm.reshape(1, D_MODEL), wt_perm)


def _perm_w_in_t(w):
    wt = w.T
    offs = np.concatenate([[0], np.cumsum(IN_SIZES)])
    fq, fk, fv, ff, mcq, mckv, mkr, bq, bk, bv, cu, cg, gate = [wt[offs[i]:offs[i + 1]] for i in range(13)]
    pad = jnp.zeros((LANES - MLA_ROPE - N_HEADS, w.shape[0]), w.dtype)
    return jnp.concatenate([gate, fq, fk, fv, bq, bk, bv, mcq, mckv, mkr, ff, pad, cu, cg], axis=0).astype(BF)


def _rope_lanes(x, c, sa, sb):
    return x * c + pltpu.roll(x, LANES - MLA_ROPE // 2, 1) * sa + pltpu.roll(x, MLA_ROPE // 2, 1) * sb


def _prep_kernel(z_ref, gq_ref, gkv_ref, wn_ref, wr_ref, wuk_ref, bf_ref, c_ref, sa_ref, sb_ref,
                 qlat_ref, qrope_ref, ckv_ref, krlf_ref):
    z = z_ref[...]
    mcq, mckv, kf = z[:, :256], z[:, 256:384], z[:, 384:512]
    qn = (mcq * lax.rsqrt(jnp.mean(mcq * mcq, axis=-1, keepdims=True) + EPS) * gq_ref[...]).astype(BF)
    q_nope = jnp.dot(qn, wn_ref[...], preferred_element_type=F32)
    q_rope = jnp.dot(qn, wr_ref[...], preferred_element_type=F32)
    c, sa, sb = c_ref[...], sa_ref[...], sb_ref[...]
    qrope_ref[...] = _rope_lanes(q_rope, c, sa, sb)
    for h in range(N_HEADS):
        qh = q_nope[:, h * MLA_NOPE:(h + 1) * MLA_NOPE].astype(BF)
        qlat_ref[:, h * MLA_KV_RANK:(h + 1) * MLA_KV_RANK] = jnp.dot(qh, wuk_ref[h], preferred_element_type=F32)
    ckv_ref[...] = mckv * lax.rsqrt(jnp.mean(mckv * mckv, axis=-1, keepdims=True) + EPS) * gkv_ref[...]
    lane = lax.broadcasted_iota(jnp.int32, kf.shape, 1)
    roped = _rope_lanes(kf, c, sa, sb)
    logf = _log_sigmoid(kf + bf_ref[...])
    krlf_ref[...] = jnp.where(lane < MLA_ROPE, roped, jnp.where(lane < MLA_ROPE + N_HEADS, logf, 0.0))


def _prep(tok, z, gq, gkv, wn, wr, wuk, bf128, rope_tabs):
    N = tok.N
    full = lambda a: pl.BlockSpec(a.shape, lambda i: (0,) * a.ndim)
    tab_spec = pl.BlockSpec((tok.tm, LANES), lambda i: (i % max(tok.nt, 1), 0)) if tok.bb == 1 else \
        pl.BlockSpec((tok.tm, LANES), lambda i: (0, 0))
    c, sa, sb = rope_tabs
    return pl.pallas_call(
        _prep_kernel, grid=(tok.n_tiles,),
        in_specs=[tok.z_spec(CH_MLA), full(gq), full(gkv), full(wn), full(wr), full(wuk), full(bf128),
                  tab_spec, tab_spec, tab_spec],
        out_specs=[tok.flat_spec(512), tok.flat_spec(128), tok.flat_spec(128), tok.flat_spec(128)],
        out_shape=[jax.ShapeDtypeStruct((N, 512), F32), jax.ShapeDtypeStruct((N, 128), F32),
                   jax.ShapeDtypeStruct((N, 128), F32), jax.ShapeDtypeStruct((N, 128), F32)],
        compiler_params=_params(("arbitrary",)), name="mla_prep",
    )(z, gq, gkv, wn, wr, wuk, bf128, c, sa, sb)


def _rope_tables(pos, reps):
    half = MLA_ROPE // 2
    inv = ROPE_THETA ** (-jnp.arange(half, dtype=F32) / half)
    ang = pos.astype(F32)[:, None] * inv[None, :]
    cos, sin = jnp.cos(ang), jnp.sin(ang)
    zero = jnp.zeros_like(sin)
    tile = lambda a, b: jnp.tile(jnp.concatenate([a, b], axis=1), (reps, LANES // MLA_ROPE))
    return tile(cos, cos), tile(-sin, zero), tile(zero, sin)


def _upper_tri():
    r = lax.broadcasted_iota(jnp.int32, (LANES, LANES), 0)
    c = lax.broadcasted_iota(jnp.int32, (LANES, LANES), 1)
    return (r <= c).astype(BF)


def _cumsum_kernel(x_ref, o_ref):
    x = x_ref[0]
    R = x.shape[0]
    within = _dot_exact01(x, _upper_tri())
    tot = jnp.broadcast_to(within[:, LANES - 1:LANES], (R, LANES))
    rr = lax.broadcasted_iota(jnp.int32, (R, R), 0)
    cc = lax.broadcasted_iota(jnp.int32, (R, R), 1)
    o_ref[0] = within + _dot01_exact((cc < rr).astype(BF), tot)


def _cumsum_time(x):
    G, R, _ = x.shape
    return pl.pallas_call(
        _cumsum_kernel, grid=(G,),
        in_specs=[pl.BlockSpec((1, R, LANES), lambda g: (g, 0, 0))],
        out_specs=pl.BlockSpec((1, R, LANES), lambda g: (g, 0, 0)),
        out_shape=jax.ShapeDtypeStruct(x.shape, F32),
        compiler_params=_params(("arbitrary",)), name="fox_cumsum",
    )(x)


def _init_stats(m_scr, l_scr, acc_scr):
    m_scr[...] = jnp.full(m_scr.shape, NEG, F32)
    l_scr[...] = jnp.zeros(l_scr.shape, F32)
    acc_scr[...] = jnp.zeros(acc_scr.shape, F32)


def _online_t(st, vt_bf, m_ref, l_ref, acc_ref, col_on=None, shift=None):
    m_prev = m_ref[...]
    m_tile = jnp.max(st, axis=0, keepdims=True)
    if shift is not None:
        m_tile = m_tile + shift
    m_new = jnp.maximum(m_prev, m_tile)
    if col_on is not None:
        m_new = jnp.where(col_on, m_new, m_prev)
    alpha = jnp.exp(m_prev - m_new)
    m_use = m_new if shift is None else m_new - shift
    if col_on is not None:
        m_use = jnp.where(col_on, m_use, -NEG)
    p = jnp.exp(st - m_use)
    l_ref[...] = alpha * l_ref[...] + jnp.sum(p, axis=0, keepdims=True)
    m_ref[...] = m_new
    acc_ref[...] = alpha * acc_ref[...] + jnp.dot(vt_bf, p.astype(BF), preferred_element_type=F32)


def _strip_plan(d, tk, tq, sw):
    for s in range(tq // sw):
        ss = slice(s * sw, (s + 1) * sw)
        if d is None:
            yield ss, None
            continue
        delta = s * sw - d * tk
        if delta <= -sw:
            continue
        yield ss, (None if delta >= tk - 1 else delta)


def _strip_width(tq):
    return tq


def _visible(tk, sw, delta):
    row = lax.broadcasted_iota(jnp.int32, (tk, sw), 0)
    col = lax.broadcasted_iota(jnp.int32, (tk, sw), 1)
    return row <= col + delta


def _attn_scratch(tq, dv):
    return [pltpu.VMEM((8, tq), F32), pltpu.VMEM((8, tq), F32), pltpu.VMEM((N_HEADS * dv, tq), F32)]


def _causal_branches(qi, ki, ratio, process):
    d = ki - ratio * qi

    @pl.when(d < 0)
    def _():
        process(None)

    for dd in range(ratio):
        @pl.when(d == dd)
        def _(dd=dd):
            process(dd)


def _fox_kernel(q_ref, k_ref, v_ref, ck_ref, cq_ref, o_ref, m_scr, l_scr, acc_scr, *, tq, tk, sw):
    qi, ki = pl.program_id(1), pl.program_id(2)

    @pl.when(ki == 0)
    def _():
        _init_stats(m_scr, l_scr, acc_scr)

    def process(d):
        q = (q_ref[:, :BRANCH_W] * QK_SCALE).astype(BF)
        k = k_ref[:, BRANCH_W:].astype(BF)
        vt = v_ref[:, :BRANCH_W].T.astype(BF)
        ck = ck_ref[0]
        cq = cq_ref[0]
        for h in range(N_HEADS):
            hs = slice(h * HEAD_DIM, (h + 1) * HEAD_DIM)
            for ss, delta in _strip_plan(d, tk, tq, sw):
                st = _dot_nt(k[:, hs], q[ss, hs]) + cq[h:h + 1, ss] - ck[:, h:h + 1]
                if delta is not None:
                    st = jnp.where(_visible(tk, sw, delta), st, NEG)
                _online_t(st, vt[hs, :], m_scr.at[h:h + 1, ss], l_scr.at[h:h + 1, ss], acc_scr.at[hs, ss])

    _causal_branches(qi, ki, tq // tk, process)

    @pl.when(ki == pl.num_programs(2) - 1)
    def _():
        ot = jnp.concatenate([acc_scr[h * HEAD_DIM:(h + 1) * HEAD_DIM, :] / l_scr[h:h + 1, :]
                              for h in range(N_HEADS)], axis=0)
        o_ref[...] = ot.T


def _fox_prompt(z, cum, cum_t, B, T, tq, tk):
    nq, nk, ratio = T // tq, T // tk, tq // tk
    last_k = lambda qi, ki: jnp.minimum(ki, ratio * qi + ratio - 1)
    zq = lambda ch: pl.BlockSpec((None, tq, CHUNK), lambda b, qi, ki: (ch, b * nq + qi, 0))
    zk = lambda ch: pl.BlockSpec((None, tk, CHUNK), lambda b, qi, ki: (ch, b * nk + last_k(qi, ki), 0))
    return pl.pallas_call(
        functools.partial(_fox_kernel, tq=tq, tk=tk, sw=_strip_width(tq)), grid=(B, nq, nk),
        in_specs=[zq(CH_FQK), zk(CH_FQK), zk(CH_FV_BQ),
                  pl.BlockSpec((1, tk, N_HEADS), lambda b, qi, ki: (b, last_k(qi, ki), 0)),
                  pl.BlockSpec((1, N_HEADS, tq), lambda b, qi, ki: (b, 0, qi))],
        out_specs=pl.BlockSpec((tq, BRANCH_W), lambda b, qi, ki: (b * nq + qi, 0)),
        out_shape=jax.ShapeDtypeStruct((B * T, BRANCH_W), F32),
        scratch_shapes=_attn_scratch(tq, HEAD_DIM),
        compiler_params=_params(("arbitrary", "arbitrary", "arbitrary")), name="fox_prompt",
    )(z, z, z, cum, cum_t)


_MLA_SCALE = (MLA_NOPE + MLA_ROPE) ** -0.5


def _mla_kernel(ql_ref, qr_ref, ckv_ref, kr_ref, wuvt_ref, o_ref, m_scr, l_scr, acc_scr, *, tq, tk, sw):
    qi, ki = pl.program_id(1), pl.program_id(2)

    @pl.when(ki == 0)
    def _():
        _init_stats(m_scr, l_scr, acc_scr)

    def process(d):
        ckv = ckv_ref[...]
        c_bf = ckv.astype(BF)
        ct_bf = ckv.T.astype(BF)
        kr = kr_ref[:, :MLA_ROPE].astype(BF)
        ql = ql_ref[...].astype(BF)
        qr = qr_ref[...].astype(BF)
        for h in range(N_HEADS):
            ls = slice(h * MLA_KV_RANK, (h + 1) * MLA_KV_RANK)
            rs = slice(h * MLA_ROPE, (h + 1) * MLA_ROPE)
            for ss, delta in _strip_plan(d, tk, tq, sw):
                st = (_dot_nt(c_bf, ql[ss, ls]) + _dot_nt(kr, qr[ss, rs])) * _MLA_SCALE
                if delta is not None:
                    st = jnp.where(_visible(tk, sw, delta), st, NEG)
                _online_t(st, ct_bf, m_scr.at[h:h + 1, ss], l_scr.at[h:h + 1, ss], acc_scr.at[ls, ss])

    _causal_branches(qi, ki, tq // tk, process)

    @pl.when(ki == pl.num_programs(2) - 1)
    def _():
        parts = []
        for h in range(N_HEADS):
            ls = slice(h * MLA_KV_RANK, (h + 1) * MLA_KV_RANK)
            o_lat_t = (acc_scr[ls, :] / l_scr[h:h + 1, :]).astype(BF)
            parts.append(jnp.dot(wuvt_ref[h], o_lat_t, preferred_element_type=F32))
        o_ref[...] = jnp.concatenate(parts, axis=0).T


def _mla_prompt(qlat, qrope, ckv, krlf, wuv_t, B, T, tq, tk):
    nq, nk, ratio = T // tq, T // tk, tq // tk
    qs = lambda w: pl.BlockSpec((tq, w), lambda b, qi, ki: (b * nq + qi, 0))
    ks = lambda w: pl.BlockSpec((tk, w), lambda b, qi, ki: (b * nk + jnp.minimum(ki, ratio * qi + ratio - 1), 0))
    return pl.pallas_call(
        functools.partial(_mla_kernel, tq=tq, tk=tk, sw=_strip_width(tq)), grid=(B, nq, nk),
        in_specs=[qs(512), qs(128), ks(128), ks(128),
                  pl.BlockSpec(wuv_t.shape, lambda b, qi, ki: (0, 0, 0))],
        out_specs=pl.BlockSpec((tq, BRANCH_W), lambda b, qi, ki: (b * nq + qi, 0)),
        out_shape=jax.ShapeDtypeStruct((B * T, BRANCH_W), F32),
        scratch_shapes=_attn_scratch(tq, MLA_KV_RANK),
        compiler_params=_params(("arbitrary", "arbitrary", "arbitrary")), name="mla_prompt",
    )(qlat, qrope, ckv, krlf, wuv_t)


def _relbias_kernel(tab_ref, qpos_ref, kpos_ref, o_ref):
    dist = jnp.maximum(qpos_ref[...] - kpos_ref[...], 0)
    tab = tab_ref[...]
    out = jnp.broadcast_to(tab[:, 0:1], dist.shape)
    for k in range(1, REL_BUCKETS):
        out = jnp.where(dist >= _REL_THR[k - 1], tab[:, k:k + 1], out)
    o_ref[...] = out


def _relbias(tab_rows, qpos, kpos):
    R, C = tab_rows.shape[0], kpos.shape[1]
    rt = min(R, 512)
    return pl.pallas_call(
        _relbias_kernel, grid=(R // rt,),
        in_specs=[pl.BlockSpec((rt, REL_BUCKETS), lambda i: (i, 0)),
                  pl.BlockSpec((rt, 1), lambda i: (i, 0)),
                  pl.BlockSpec((1, C), lambda i: (0, 0))],
        out_specs=pl.BlockSpec((rt, C), lambda i: (i, 0)),
        out_shape=jax.ShapeDtypeStruct((R, C), F32),
        compiler_params=_params(("arbitrary",)), name="rel_bias",
    )(tab_rows, qpos, kpos)


def _kmean_kernel(z_ref, o_ref):
    o_ref[0] = jnp.mean(z_ref[:, :BRANCH_W], axis=0, keepdims=True)


def _kmean_prompt(z, N):
    nb = N // MOBA_BLOCK
    return pl.pallas_call(
        _kmean_kernel, grid=(nb,),
        in_specs=[pl.BlockSpec((None, MOBA_BLOCK, CHUNK), lambda i: (CH_BKV, i, 0))],
        out_specs=pl.BlockSpec((1, 1, BRANCH_W), lambda i: (i, 0, 0)),
        out_shape=jax.ShapeDtypeStruct((nb, 1, BRANCH_W), F32),
        compiler_params=_params(("arbitrary",)), name="moba_kmean",
    )(z)


def _top_blocks(gate, n_valid, n_blocks):
    lane = lax.broadcasted_iota(jnp.int32, gate.shape, 1)
    rank = jnp.zeros(gate.shape, F32)
    for m in range(n_blocks):
        gm = gate[:, m:m + 1]
        ahead = jnp.where(gm > gate, 1.0, jnp.where(gm == gate, jnp.where(lane > m, 1.0, 0.0), 0.0))
        rank = rank + ahead * jnp.where(m < n_valid, 1.0, 0.0)
    return jnp.where(lane < n_valid, jnp.where(rank < MOBA_TOPK, 1.0, 0.0), 0.0)


def _top_blocks_t(gate_t, n_valid, n_blocks):
    row = lax.broadcasted_iota(jnp.int32, gate_t.shape, 0)
    rank = jnp.zeros(gate_t.shape, F32)
    for m in range(n_blocks):
        gm = gate_t[m:m + 1, :]
        ahead = jnp.where(gm > gate_t, 1.0, jnp.where(gm == gate_t, jnp.where(row > m, 1.0, 0.0), 0.0))
        rank = rank + ahead * jnp.where(m < n_valid, 1.0, 0.0)
    return jnp.where(row < n_valid, jnp.where(rank < MOBA_TOPK, 1.0, 0.0), 0.0)


def _moba_kernel(q_ref, kv_ref, km_ref, bias_ref, o_ref, m_scr, l_scr, acc_scr, sel_scr, *, nb, sw):
    qi, ki = pl.program_id(1), pl.program_id(2)
    tq = tk = MOBA_BLOCK

    @pl.when(ki == 0)
    def _():
        _init_stats(m_scr, l_scr, acc_scr)
        q = q_ref[:, BRANCH_W:]
        km = km_ref[0]
        for h in range(N_HEADS):
            hs = slice(h * HEAD_DIM, (h + 1) * HEAD_DIM)
            g_t = _dot_f32(km[:, hs], q[:, hs], _NT)
            sel_t = _top_blocks_t(g_t, qi, nb)
            for n in range(nb):
                sel_scr[h, n] = sel_t[n:n + 1, :]

    def process(kind):
        q = (q_ref[:, BRANCH_W:] * QK_SCALE).astype(BF)
        k = kv_ref[:, :BRANCH_W].astype(BF)
        vt = kv_ref[:, BRANCH_W:].T.astype(BF)
        for h in range(N_HEADS):
            hs = slice(h * HEAD_DIM, (h + 1) * HEAD_DIM)
            for s in range(tq // sw):
                ss = slice(s * sw, (s + 1) * sw)
                st = _dot_nt(k[:, hs], q[ss, hs])
                stats = (m_scr.at[h:h + 1, ss], l_scr.at[h:h + 1, ss], acc_scr.at[hs, ss])
                if kind == "diag":
                    st = jnp.where(_visible(tk, sw, s * sw), st + bias_ref[0, h, :, ss], NEG)
                    _online_t(st, vt[hs, :], *stats)
                    continue
                picked = sel_scr[h, ki][:, ss] > 0.0
                if kind == "near":
                    _online_t(st + bias_ref[0, h, :, ss], vt[hs, :], *stats, col_on=picked)
                else:
                    _online_t(st, vt[hs, :], *stats, col_on=picked, shift=bias_ref[0, h, 0:1, 0:1])

    @pl.when(ki < qi - 1)
    def _():
        process("far")

    @pl.when(ki == qi - 1)
    def _():
        process("near")

    @pl.when(ki == qi)
    def _():
        process("diag")

    @pl.when(ki == pl.num_programs(2) - 1)
    def _():
        ot = jnp.concatenate([acc_scr[h * HEAD_DIM:(h + 1) * HEAD_DIM, :] / l_scr[h:h + 1, :]
                              for h in range(N_HEADS)], axis=0)
        o_ref[...] = ot.T


def _moba_prompt(z, kmean, bias, B, T):
    tq = MOBA_BLOCK
    nq = T // tq
    assert nq <= LANES
    return pl.pallas_call(
        functools.partial(_moba_kernel, nb=nq, sw=_strip_width(tq)), grid=(B, nq, nq),
        in_specs=[pl.BlockSpec((None, tq, CHUNK), lambda b, qi, ki: (CH_FV_BQ, b * nq + qi, 0)),
                  pl.BlockSpec((None, tq, CHUNK), lambda b, qi, ki: (CH_BKV, b * nq + jnp.minimum(ki, qi), 0)),
                  pl.BlockSpec((1, nq, BRANCH_W), lambda b, qi, ki: (b, 0, 0)),
                  pl.BlockSpec((1, N_HEADS, tq, tq),
                               lambda b, qi, ki: (jnp.clip(qi - ki, 0, 2), 0, 0, 0))],
        out_specs=pl.BlockSpec((tq, BRANCH_W), lambda b, qi, ki: (b * nq + qi, 0)),
        out_shape=jax.ShapeDtypeStruct((B * T, BRANCH_W), F32),
        scratch_shapes=_attn_scratch(tq, HEAD_DIM) + [pltpu.VMEM((N_HEADS, nq, 1, tq), F32)],
        compiler_params=_params(("arbitrary", "arbitrary", "arbitrary")), name="moba_prompt",
    )(z, z, kmean, bias)


def _head_rows(x, width):
    return jnp.concatenate([x[:, h * width:(h + 1) * width] for h in range(N_HEADS)], axis=0)


def _blockdiag_rows(q):
    lane_head = lax.broadcasted_iota(jnp.int32, q.shape, 1) // HEAD_DIM
    return jnp.concatenate([jnp.where(lane_head == h, q, 0.0) for h in range(N_HEADS)], axis=0)


def _diag_heads(o_all, tq):
    lane_head = lax.broadcasted_iota(jnp.int32, (tq, o_all.shape[1]), 1) // HEAD_DIM
    out = jnp.zeros((tq, o_all.shape[1]), F32)
    for h in range(N_HEADS):
        out = out + jnp.where(lane_head == h, o_all[h * tq:(h + 1) * tq], 0.0)
    return out


def _rows_per_head(x, tq):
    return jnp.concatenate([jnp.broadcast_to(x[h:h + 1], (tq, x.shape[1])) for h in range(N_HEADS)], axis=0)


def _page_specs(block, l, pg, np_):
    zeros = (0,) * (len(block) - 2)
    return [pl.BlockSpec(block, lambda b, s, pt, p=p: (l, pt[b * np_ + s * pg + p]) + zeros) for p in range(pg)]


_KT_BLOCK = (None, None, N_HEADS, HEAD_DIM, PAGE)


def _kt(ref):
    return ref[...].reshape(BRANCH_W, PAGE).astype(BF)


def _new_rows_mask(R, tq):
    row_t = lax.broadcasted_iota(jnp.int32, (R, LANES), 0) % tq
    col = lax.broadcasted_iota(jnp.int32, (R, LANES), 1)
    return col <= row_t


def _tile_update(s_all, pv_fn, m_scr, l_scr, acc_scr):
    m_prev = m_scr[...][:, :1]
    m_new = jnp.maximum(m_prev, jnp.max(s_all, axis=-1, keepdims=True))
    alpha = jnp.exp(m_prev - m_new)
    p = jnp.exp(s_all - m_new)
    l_new = alpha * l_scr[...][:, :1] + jnp.sum(p, axis=-1, keepdims=True)
    l_scr[...] = jnp.broadcast_to(l_new, l_scr.shape)
    m_scr[...] = jnp.broadcast_to(m_new, m_scr.shape)
    acc_scr[...] = alpha * acc_scr[...] + pv_fn(p.astype(BF))


def _fox_dec_kernel(pt_ref, zq_ref, zv_ref, lfn_ref, *rest, pg, tq):
    k_refs, v_refs, lf_refs = rest[:pg], rest[pg:2 * pg], rest[2 * pg:3 * pg]
    o_ref = rest[3 * pg]
    qbd, m_scr, l_scr, acc_scr, run_scr, padk, padv = rest[3 * pg + 1:]
    s_idx = pl.program_id(1)
    n_steps = pl.num_programs(1)
    R = N_HEADS * tq
    G = pg * N_HEADS

    @pl.when(s_idx == 0)
    def _():
        _init_stats(m_scr, l_scr, acc_scr)
        qbd[...] = _blockdiag_rows(zq_ref[:, :BRANCH_W] * QK_SCALE).astype(BF)
        run_scr[...] = jnp.zeros(run_scr.shape, F32)

    def tile(with_new):
        upper = _upper_tri()
        lf = jnp.concatenate([lf_refs[p][...] for p in range(pg)], axis=0)
        within = _dot_exact01(lf, upper)
        tot = jnp.broadcast_to(within[:, LANES - 1:LANES], (G, LANES))
        r = lax.broadcasted_iota(jnp.int32, (G, G), 0)
        c = lax.broadcasted_iota(jnp.int32, (G, G), 1)
        same_head = (r % N_HEADS) == (c % N_HEADS)
        earlier = jnp.where(same_head, jnp.where(c // N_HEADS < r // N_HEADS, 1.0, 0.0), 0.0).astype(BF)
        run = run_scr[...]
        fk = run + _dot01_exact(earlier, tot) + within
        run_new = run + _dot01_exact(jnp.where(same_head, 1.0, 0.0).astype(BF), tot)
        run_scr[...] = run_new
        q = qbd[...]
        parts = [jnp.dot(q, _kt(k_refs[p]), preferred_element_type=F32)
                 - _rows_per_head(fk[p * N_HEADS:(p + 1) * N_HEADS], tq) for p in range(pg)]
        if with_new:
            lfn8 = jnp.concatenate([lfn_ref[...], jnp.zeros((8 - N_HEADS, LANES), F32)], axis=0)
            fkn = run_new[0:8] + _dot_exact01(lfn8, upper)
            s_new = _dot_nt(q, padk[...].astype(BF)) - _rows_per_head(fkn, tq)
            parts.append(jnp.where(_new_rows_mask(R, tq), s_new, NEG))

        def pv_fn(p):
            out = None
            for i in range(pg):
                t = _dot_nt(p[:, i * PAGE:(i + 1) * PAGE], _kt(v_refs[i]))
                out = t if out is None else out + t
            if with_new:
                out = out + jnp.dot(p[:, pg * PAGE:], padv[...].astype(BF), preferred_element_type=F32)
            return out

        _tile_update(jnp.concatenate(parts, axis=1), pv_fn, m_scr, l_scr, acc_scr)

    @pl.when(s_idx < n_steps - 1)
    def _():
        tile(False)

    @pl.when(s_idx == n_steps - 1)
    def _():
        padk[...] = jnp.zeros(padk.shape, F32)
        padv[...] = jnp.zeros(padv.shape, F32)
        padk[0:tq, :] = zq_ref[:, BRANCH_W:]
        padv[0:tq, :] = zv_ref[:, :BRANCH_W]
        tile(True)
        o_ref[...] = _diag_heads(acc_scr[...] / l_scr[...][:, :1], tq)


def _fox_decode(z, lf_new_t, pool_kt, pool_vt, pool_lft, pt_flat, l, B, tq, np_, pg):
    R = N_HEADS * tq
    grid_spec = pltpu.PrefetchScalarGridSpec(
        num_scalar_prefetch=1, grid=(B, np_ // pg),
        in_specs=[pl.BlockSpec((None, tq, CHUNK), lambda b, s, pt: (CH_FQK, b, 0)),
                  pl.BlockSpec((None, tq, CHUNK), lambda b, s, pt: (CH_FV_BQ, b, 0)),
                  pl.BlockSpec((None, N_HEADS, LANES), lambda b, s, pt: (b, 0, 0))]
        + _page_specs(_KT_BLOCK, l, pg, np_) + _page_specs(_KT_BLOCK, l, pg, np_)
        + _page_specs((None, None, N_HEADS, PAGE), l, pg, np_),
        out_specs=pl.BlockSpec((tq, BRANCH_W), lambda b, s, pt: (b, 0)),
        scratch_shapes=[pltpu.VMEM((R, BRANCH_W), BF), pltpu.VMEM((R, LANES), F32), pltpu.VMEM((R, LANES), F32),
                        pltpu.VMEM((R, BRANCH_W), F32), pltpu.VMEM((pg * N_HEADS, LANES), F32),
                        pltpu.VMEM((PAGE, BRANCH_W), F32), pltpu.VMEM((PAGE, BRANCH_W), F32)])
    return pl.pallas_call(
        functools.partial(_fox_dec_kernel, pg=pg, tq=tq), grid_spec=grid_spec,
        out_shape=jax.ShapeDtypeStruct((B * tq, BRANCH_W), F32),
        compiler_params=_params(("arbitrary", "arbitrary")), name="fox_decode",
    )(pt_flat, z, z, lf_new_t, *([pool_kt] * pg), *([pool_vt] * pg), *([pool_lft] * pg))


def _mla_dec_kernel(pt_ref, ql_ref, qr_ref, cn_ref, kn_ref, wuv_ref, *rest, pg, tq):
    c_refs, r_refs = rest[:pg], rest[pg:2 * pg]
    o_ref = rest[2 * pg]
    ql_scr, qr_scr, m_scr, l_scr, acc_scr, padc, padr = rest[2 * pg + 1:]
    s_idx = pl.program_id(1)
    n_steps = pl.num_programs(1)
    R = N_HEADS * tq

    @pl.when(s_idx == 0)
    def _():
        _init_stats(m_scr, l_scr, acc_scr)
        ql_scr[...] = _head_rows(ql_ref[...], MLA_KV_RANK).astype(BF)
        qr_scr[...] = _head_rows(qr_ref[...], MLA_ROPE).astype(BF)

    def tile(with_new):
        ql, qr = ql_scr[...], qr_scr[...]
        cs = [c_refs[p][...].astype(BF) for p in range(pg)]
        parts = [(_dot_nt(ql, cs[p]) + jnp.dot(qr, r_refs[p][...].astype(BF), preferred_element_type=F32))
                 * _MLA_SCALE for p in range(pg)]
        if with_new:
            s_new = (_dot_nt(ql, padc[...].astype(BF)) + _dot_nt(qr, padr[...].astype(BF))) * _MLA_SCALE
            parts.append(jnp.where(_new_rows_mask(R, tq), s_new, NEG))

        def pv_fn(p):
            out = None
            for i in range(pg):
                t = jnp.dot(p[:, i * PAGE:(i + 1) * PAGE], cs[i], preferred_element_type=F32)
                out = t if out is None else out + t
            if with_new:
                out = out + jnp.dot(p[:, pg * PAGE:], padc[...].astype(BF), preferred_element_type=F32)
            return out

        _tile_update(jnp.concatenate(parts, axis=1), pv_fn, m_scr, l_scr, acc_scr)

    @pl.when(s_idx < n_steps - 1)
    def _():
        tile(False)

    @pl.when(s_idx == n_steps - 1)
    def _():
        padc[...] = jnp.zeros(padc.shape, F32)
        padr[...] = jnp.zeros(padr.shape, F32)
        padc[0:tq, :] = cn_ref[...]
        padr[0:tq, :] = kn_ref[:, :MLA_ROPE]
        tile(True)
        o_lat = (acc_scr[...] / l_scr[...][:, :1]).astype(BF)
        for h in range(N_HEADS):
            o_ref[:, h * HEAD_DIM:(h + 1) * HEAD_DIM] = jnp.dot(
                o_lat[h * tq:(h + 1) * tq], wuv_ref[h], preferred_element_type=F32)


def _mla_decode(qlat, qrope, ckv, krlf, wuv, pool_c, pool_rt, pt_flat, l, B, tq, np_, pg):
    R = N_HEADS * tq
    row = lambda w: pl.BlockSpec((tq, w), lambda b, s, pt: (b, 0))
    grid_spec = pltpu.PrefetchScalarGridSpec(
        num_scalar_prefetch=1, grid=(B, np_ // pg),
        in_specs=[row(512), row(128), row(128), row(128),
                  pl.BlockSpec(wuv.shape, lambda b, s, pt: (0, 0, 0))]
        + _page_specs((None, None, PAGE, MLA_KV_RANK), l, pg, np_)
        + _page_specs((None, None, MLA_ROPE, PAGE), l, pg, np_),
        out_specs=pl.BlockSpec((tq, BRANCH_W), lambda b, s, pt: (b, 0)),
        scratch_shapes=[pltpu.VMEM((R, MLA_KV_RANK), BF), pltpu.VMEM((R, MLA_ROPE), BF),
                        pltpu.VMEM((R, LANES), F32), pltpu.VMEM((R, LANES), F32),
                        pltpu.VMEM((R, MLA_KV_RANK), F32),
                        pltpu.VMEM((PAGE, MLA_KV_RANK), F32), pltpu.VMEM((PAGE, MLA_ROPE), F32)])
    return pl.pallas_call(
        functools.partial(_mla_dec_kernel, pg=pg, tq=tq), grid_spec=grid_spec,
        out_shape=jax.ShapeDtypeStruct((B * tq, BRANCH_W), F32),
        compiler_params=_params(("arbitrary", "arbitrary")), name="mla_decode",
    )(pt_flat, qlat, qrope, ckv, krlf, wuv, *([pool_c] * pg), *([pool_rt] * pg))


def _moba_dec_kernel(pt_ref, zq_ref, zkv_ref, bias_ref, *rest, pg, tq, nb):
    k_refs, v_refs = rest[:pg], rest[pg:2 * pg]
    o_ref = rest[2 * pg]
    qbd, qf_scr, km_scr, mb_scr, lb_scr, ob_scr, padk, padv = rest[2 * pg + 1:]
    s_idx = pl.program_id(1)
    R = N_HEADS * tq

    @pl.when(s_idx == 0)
    def _():
        q = _blockdiag_rows(zq_ref[:, BRANCH_W:] * QK_SCALE)
        qf_scr[...] = q
        qbd[...] = q.astype(BF)
        km_scr[...] = jnp.zeros(km_scr.shape, F32)

    def block_stats(s_parts, pv_fn):
        m = s_parts[0].max(axis=-1, keepdims=True)
        for s in s_parts[1:]:
            m = jnp.maximum(m, s.max(axis=-1, keepdims=True))
        ps = [jnp.exp(s - m) for s in s_parts]
        lsum = ps[0].sum(axis=-1, keepdims=True)
        for p in ps[1:]:
            lsum = lsum + p.sum(axis=-1, keepdims=True)
        return m, lsum, pv_fn([p.astype(BF) for p in ps])

    b_far = bias_ref[2 * R:3 * R, :]
    b_last = bias_ref[R:2 * R, :]
    lane = lax.broadcasted_iota(jnp.int32, (BRANCH_W, LANES), 1)
    for j in range(pg // 2):
        n = s_idx * (pg // 2) + j
        kta, ktb = k_refs[2 * j][...].reshape(BRANCH_W, PAGE), k_refs[2 * j + 1][...].reshape(BRANCH_W, PAGE)
        kmean = jnp.sum(kta + ktb, axis=-1, keepdims=True) * (1.0 / MOBA_BLOCK)
        km_scr[...] = jnp.where(lane == n, kmean, km_scr[...])
        bias = jnp.where(n == nb - 1, b_last, b_far)
        q = qbd[...]
        s_a = jnp.dot(q, kta.astype(BF), preferred_element_type=F32) + bias[:, :PAGE]
        s_b = jnp.dot(q, ktb.astype(BF), preferred_element_type=F32) + bias[:, PAGE:]
        m, lsum, o = block_stats(
            [s_a, s_b], lambda ps, j=j: _dot_nt(ps[0], _kt(v_refs[2 * j])) + _dot_nt(ps[1], _kt(v_refs[2 * j + 1])))
        mb_scr[n] = jnp.broadcast_to(m, (R, LANES))
        lb_scr[n] = jnp.broadcast_to(lsum, (R, LANES))
        ob_scr[n] = o

    @pl.when(s_idx == pl.num_programs(1) - 1)
    def _():
        padk[...] = jnp.zeros(padk.shape, F32)
        padv[...] = jnp.zeros(padv.shape, F32)
        padk[0:tq, :] = zkv_ref[:, :BRANCH_W]
        padv[0:tq, :] = zkv_ref[:, BRANCH_W:]
        s_own = jnp.where(_new_rows_mask(R, tq),
                          _dot_nt(qbd[...], padk[...].astype(BF)) + bias_ref[0:R, :PAGE], NEG)
        m_tot, l_tot, o_tot = block_stats(
            [s_own], lambda ps: jnp.dot(ps[0], padv[...].astype(BF), preferred_element_type=F32))
        g = _dot_f32(qf_scr[...], km_scr[...], _NN)
        sel = _top_blocks(g, nb, nb)
        for n in range(nb):
            on = sel[:, n:n + 1] > 0.0
            m_n = jnp.where(on, mb_scr[n][:, :1], NEG)
            m_new = jnp.maximum(m_tot, m_n)
            a_old = jnp.exp(m_tot - m_new)
            a_n = jnp.where(on, jnp.exp(m_n - m_new), 0.0)
            l_tot = a_old * l_tot + a_n * lb_scr[n][:, :1]
            o_tot = a_old * o_tot + a_n * ob_scr[n]
            m_tot = m_new
        o_ref[...] = _diag_heads(o_tot / l_tot, tq)


def _moba_decode(z, bias, pool_kt, pool_vt, pt_flat, l, B, tq, np_, pg):
    R = N_HEADS * tq
    nb = np_ * PAGE // MOBA_BLOCK
    assert pg % 2 == 0 and nb <= LANES
    grid_spec = pltpu.PrefetchScalarGridSpec(
        num_scalar_prefetch=1, grid=(B, np_ // pg),
        in_specs=[pl.BlockSpec((None, tq, CHUNK), lambda b, s, pt: (CH_FV_BQ, b, 0)),
                  pl.BlockSpec((None, tq, CHUNK), lambda b, s, pt: (CH_BKV, b, 0)),
                  pl.BlockSpec(bias.shape, lambda b, s, pt: (0, 0))]
        + _page_specs(_KT_BLOCK, l, pg, np_) + _page_specs(_KT_BLOCK, l, pg, np_),
        out_specs=pl.BlockSpec((tq, BRANCH_W), lambda b, s, pt: (b, 0)),
        scratch_shapes=[pltpu.VMEM((R, BRANCH_W), BF), pltpu.VMEM((R, BRANCH_W), F32),
                        pltpu.VMEM((BRANCH_W, LANES), F32),
                        pltpu.VMEM((nb, R, LANES), F32), pltpu.VMEM((nb, R, LANES), F32),
                        pltpu.VMEM((nb, R, BRANCH_W), F32),
                        pltpu.VMEM((PAGE, BRANCH_W), F32), pltpu.VMEM((PAGE, BRANCH_W), F32)])
    return pl.pallas_call(
        functools.partial(_moba_dec_kernel, pg=pg, tq=tq, nb=nb), grid_spec=grid_spec,
        out_shape=jax.ShapeDtypeStruct((B * tq, BRANCH_W), F32),
        compiler_params=_params(("arbitrary", "arbitrary")), name="moba_decode",
    )(pt_flat, z, z, bias, *([pool_kt] * pg), *([pool_vt] * pg))


_HALO = 32


def _conv_kernel(z_ref, hist_ref, w_ref, b_ref, g_ref, beta_ref, y_ref, buf_ref, ext, *, tt):
    ti = pl.program_id(1)

    @pl.when(ti == 0)
    def _():
        ext[0:_HALO, :] = hist_ref[0]

    @pl.when(ti > 0)
    def _():
        ext[0:_HALO, :] = ext[tt:tt + _HALO, :]

    z = z_ref[...]
    ext[_HALO:_HALO + tt, :] = z[:, :BRANCH_W] * jax.nn.sigmoid(z[:, BRANCH_W:])
    first = _HALO - (CONV_W - 1)
    y = jnp.zeros((tt, BRANCH_W), F32)
    for j in range(CONV_W):
        y = y + w_ref[j:j + 1, :] * ext[first + j:first + j + tt, :]
    y = y + b_ref[...]
    mu = jnp.mean(y, axis=-1, keepdims=True)
    var = jnp.mean(jnp.square(y - mu), axis=-1, keepdims=True)
    yn = (y - mu) * lax.rsqrt(var + EPS) * g_ref[...] + beta_ref[...]
    y_ref[...] = yn * jax.nn.sigmoid(yn)

    @pl.when(ti == pl.num_programs(1) - 1)
    def _():
        buf_ref[0] = ext[tt + first:tt + _HALO, :]


def _conv(z, hist32, w_dw, b_dw, ln_g, ln_b, B, T, tt):
    nt = T // tt
    vec = lambda: pl.BlockSpec((1, BRANCH_W), lambda b, t: (0, 0))
    return pl.pallas_call(
        functools.partial(_conv_kernel, tt=tt), grid=(B, nt),
        in_specs=[pl.BlockSpec((None, tt, CHUNK), lambda b, t: (CH_CONV, b * nt + t, 0)),
                  pl.BlockSpec((1, _HALO, BRANCH_W), lambda b, t: (b, 0, 0)),
                  pl.BlockSpec((CONV_W, BRANCH_W), lambda b, t: (0, 0)), vec(), vec(), vec()],
        out_specs=[pl.BlockSpec((tt, BRANCH_W), lambda b, t: (b * nt + t, 0)),
                   pl.BlockSpec((1, CONV_W - 1, BRANCH_W), lambda b, t: (b, 0, 0))],
        out_shape=[jax.ShapeDtypeStruct((B * T, BRANCH_W), F32),
                   jax.ShapeDtypeStruct((B, CONV_W - 1, BRANCH_W), F32)],
        scratch_shapes=[pltpu.VMEM((tt + _HALO, BRANCH_W), F32)],
        compiler_params=_params(("arbitrary", "arbitrary")), name="conv_module",
    )(z, hist32, w_dw, b_dw.reshape(1, -1), ln_g.reshape(1, -1), ln_b.reshape(1, -1))


def _merge_kernel(x_ref, g1_ref, gate_ref, of_ref, om_ref, ob_ref, oc_ref, wb_ref, wo_ref, o_ref):
    merged = None
    for n, br in enumerate((of_ref, om_ref, ob_ref, oc_ref)):
        proj = jnp.dot(br[...].astype(BF), wb_ref[n], preferred_element_type=F32)
        sig = jnp.concatenate([gate_ref[2 * n], gate_ref[2 * n + 1]], axis=-1)
        merged = sig * proj if merged is None else merged + sig * proj
    out = jnp.dot(merged.astype(BF), wo_ref[...], preferred_element_type=F32)
    o_ref[...] = x_ref[...] + g1_ref[...] * out.reshape(x_ref.shape)


def _merge(tok, x, mod, l, z, o_fox, o_mla, o_moba, y_conv, w_branch, w_out):
    return pl.pallas_call(
        _merge_kernel, grid=(tok.n_tiles,),
        in_specs=[tok.x_spec(), tok.mod_spec(l, 2), tok.z_spec(0, 8)]
        + [tok.flat_spec(BRANCH_W)] * 4
        + [pl.BlockSpec(w_branch.shape, lambda i: (0, 0, 0)), pl.BlockSpec(w_out.shape, lambda i: (0, 0))],
        out_specs=tok.x_spec(),
        out_shape=jax.ShapeDtypeStruct(x.shape, F32),
        compiler_params=_params(("arbitrary",)), name="merge",
    )(x, mod, z, o_fox, o_mla, o_moba, y_conv, w_branch, w_out)


def _ffn_kernel(x_ref, sh_ref, sc_ref, g2_ref, gn_ref, wr_ref, br_ref, wg_ref, wu_ref, wd_ref, gout_ref,
                o_ref, h_scr, comb_scr, acc_e, acc_o, *, n_exp, final):
    e, f = pl.program_id(1), pl.program_id(2)
    nf = pl.num_programs(2)
    tm = h_scr.shape[0]

    @pl.when((e == 0) & (f == 0))
    def _():
        h = _modulated_norm(x_ref[...], gn_ref[...], sc_ref[...], sh_ref[...]).reshape(h_scr.shape)
        h_scr[...] = h.astype(BF)
        acc_o[...] = jnp.zeros(acc_o.shape, F32)
        if n_exp > 1:
            lane = lax.broadcasted_iota(jnp.int32, (tm, LANES), 1)
            logits = _dot_f32(h, wr_ref[...], _NT) + br_ref[...]
            logits = jnp.where(lane < n_exp, logits, NEG)
            m1 = jnp.max(logits, axis=-1, keepdims=True)
            i1 = jnp.min(jnp.where(logits == m1, lane, LANES), axis=-1, keepdims=True)
            rest = jnp.where(lane == i1, NEG, logits)
            m2 = jnp.max(rest, axis=-1, keepdims=True)
            i2 = jnp.min(jnp.where(rest == m2, lane, LANES), axis=-1, keepdims=True)
            e2 = jnp.exp(m2 - m1)
            w1 = 1.0 / (1.0 + e2)
            w2 = e2 / (1.0 + e2)
            comb_scr[...] = jnp.where(lane == i1, w1, 0.0) + jnp.where(lane == i2, w2, 0.0)

    @pl.when(f == 0)
    def _():
        acc_e[...] = jnp.zeros(acc_e.shape, F32)

    h = h_scr[...]
    a = jnp.dot(h, wg_ref[0], preferred_element_type=F32)
    u = jnp.dot(h, wu_ref[0], preferred_element_type=F32)
    act = (a * jax.nn.sigmoid(a) * u).astype(BF)
    acc_e[...] += jnp.dot(act, wd_ref[0], preferred_element_type=F32)

    @pl.when(f == nf - 1)
    def _():
        if n_exp > 1:
            lane = lax.broadcasted_iota(jnp.int32, (tm, LANES), 1)
            w = jnp.sum(jnp.where(lane == e, comb_scr[...], 0.0), axis=-1, keepdims=True)
            acc_o[...] += w * acc_e[...]
        else:
            acc_o[...] += acc_e[...]

    @pl.when((e == n_exp - 1) & (f == nf - 1))
    def _():
        xn = x_ref[...] + g2_ref[...] * acc_o[...].reshape(x_ref.shape)
        if final:
            ms = jnp.mean(xn * xn, axis=-1, keepdims=True)
            xn = xn * lax.rsqrt(ms + EPS) * gout_ref[...]
        o_ref[...] = xn


def _ffn(tok, x, mod, l, g_norm, w_router_t, b_router, wg, wu, wd, g_out, tf, final):
    n_exp, _, F = wg.shape
    nf = F // tf
    assert F % tf == 0 and tf % LANES == 0
    return pl.pallas_call(
        functools.partial(_ffn_kernel, n_exp=n_exp, final=final), grid=(tok.n_tiles, n_exp, nf),
        in_specs=[tok.x_spec(), tok.mod_spec(l, 3), tok.mod_spec(l, 4), tok.mod_spec(l, 5),
                  pl.BlockSpec((1, D_MODEL), lambda i, e, f: (0, 0)),
                  pl.BlockSpec(w_router_t.shape, lambda i, e, f: (0, 0)),
                  pl.BlockSpec((1, LANES), lambda i, e, f: (0, 0)),
                  pl.BlockSpec((1, D_MODEL, tf), lambda i, e, f: (e, 0, f)),
                  pl.BlockSpec((1, D_MODEL, tf), lambda i, e, f: (e, 0, f)),
                  pl.BlockSpec((1, tf, D_MODEL), lambda i, e, f: (e, f, 0)),
                  pl.BlockSpec((1, D_MODEL), lambda i, e, f: (0, 0))],
        out_specs=tok.x_spec(),
        out_shape=jax.ShapeDtypeStruct(x.shape, F32),
        scratch_shapes=[pltpu.VMEM((tok.tm, D_MODEL), BF), pltpu.VMEM((tok.tm, LANES), F32),
                        pltpu.VMEM((tok.tm, D_MODEL), F32), pltpu.VMEM((tok.tm, D_MODEL), F32)],
        compiler_params=_params(("arbitrary", "arbitrary", "arbitrary")), name="ffn",
    )(x, mod, mod, mod, g_norm.reshape(1, D_MODEL), w_router_t, b_router, wg, wu, wd, g_out.reshape(1, D_MODEL))


def _largest_tile(n, cap):
    t = min(n, cap)
    while n % t:
        t //= 2
    return t


def _ffn_tile(F):
    for tf in (512, 1408, 896, 256, 128):
        if F % tf == 0:
            return tf
    raise ValueError(F)


def kernel(x_prompt, x_sample, cache_fox_k, cache_fox_v, cache_fox_logf, cache_mla_ckv, cache_mla_krope,
           cache_moba_k, cache_moba_v, state_conv, page_table, c_prompt, c_sample,
           w_ada, b_ada, norm_mix, norm_ffn, w_in, b_fox_f, g_mla_q, g_mla_kv, w_mla_uq, w_mla_ukv,
           w_dw, b_dw, conv_ln_g, conv_ln_b, w_branch, w_out, rel_bias,
           ffn_w_gate, ffn_w_up, ffn_w_down, moe_w_router, moe_b_router, moe_w_gate, moe_w_up, moe_w_down,
           norm_out):
    Bp, Tp, D = x_prompt.shape
    Bs, Ts, _ = x_sample.shape
    L = w_ada.shape[0]
    np_ = page_table.shape[1]
    past_len = np_ * PAGE
    assert D == D_MODEL and Tp % MOBA_BLOCK == 0 and past_len % MOBA_BLOCK == 0 and Ts == 8
    pg = _largest_tile(np_, 32)

    n_c = Bp + Bs
    mc = -(-n_c // 8) * 8
    c_all = jnp.concatenate([c_prompt, c_sample, jnp.zeros((mc - n_c, D), F32)], axis=0)
    mod_all = _ada(c_all, w_ada, b_ada)
    mod_p = mod_all[:, :, :Bp, None, :]
    mod_s = mod_all[:, :, Bp:Bp + Bs, None, :]

    tok_p = _Tok(Bp, Tp, 1, _largest_tile(Tp, 512))
    tok_p_in = _Tok(Bp, Tp, 1, _largest_tile(Tp, 1024))
    tok_s = _Tok(Bs, Ts, _largest_tile(Bs, 64), Ts)
    tq_p = _largest_tile(Tp, 512)
    tk_p = _largest_tile(tq_p, 256)

    rope_p = _rope_tables(jnp.arange(Tp), 1)
    rope_s = _rope_tables(past_len + jnp.arange(Ts), tok_s.bb)

    tab = rel_bias.T.astype(F32)
    blk = MOBA_BLOCK
    tab_rows_p = jnp.tile(jnp.repeat(tab, blk, axis=0), (3, 1))
    kneg_p = (jnp.arange(3)[:, None, None] * blk - jnp.arange(blk)[None, None, :]
              + jnp.zeros((1, N_HEADS, 1), jnp.int32)).reshape(-1, 1).astype(jnp.int32)
    bias_p = _relbias(tab_rows_p, kneg_p, -jnp.arange(blk, dtype=jnp.int32)[None, :])
    bias_p = bias_p.reshape(3, N_HEADS, blk, blk)
    tab_rows_s = jnp.tile(jnp.repeat(tab, Ts, axis=0), (3, 1))
    qpos_s = (jnp.arange(3)[:, None, None] * blk + jnp.arange(Ts)[None, None, :]
              + jnp.zeros((1, N_HEADS, 1), jnp.int32)).reshape(-1, 1).astype(jnp.int32)
    bias_s = _relbias(tab_rows_s, qpos_s, jnp.arange(blk, dtype=jnp.int32)[None, :])

    pt_flat = page_table.reshape(-1).astype(jnp.int32)
    page_t = lambda a: jnp.transpose(a, (0, 1, 3, 4, 2))
    pk_fox, pv_fox = page_t(cache_fox_k), page_t(cache_fox_v)
    pk_moba, pv_moba = page_t(cache_moba_k), page_t(cache_moba_v)
    p_lft = jnp.swapaxes(cache_fox_logf, 2, 3)
    p_krt = jnp.swapaxes(cache_mla_krope, 2, 3)
    hist_p = jnp.zeros((Bp, _HALO, BRANCH_W), F32)
    hist_s = jnp.pad(state_conv, ((0, 0), (0, 0), (_HALO - (CONV_W - 1), 0), (0, 0)))

    xp, xs = x_prompt, x_sample
    rows_p, rows_s = [], []
    for l in range(L):
        wt_perm = _perm_w_in_t(w_in[l])
        w_uq = w_mla_uq[l]
        wn = w_uq[:, :, :MLA_NOPE].reshape(MLA_Q_RANK, -1).astype(BF)
        wr = w_uq[:, :, MLA_NOPE:].reshape(MLA_Q_RANK, -1).astype(BF)
        wuk = jnp.transpose(w_mla_ukv[l][:, :, :MLA_NOPE], (1, 2, 0)).astype(BF)
        wuv = jnp.transpose(w_mla_ukv[l][:, :, MLA_NOPE:], (1, 0, 2)).astype(BF)
        wuv_t = jnp.transpose(w_mla_ukv[l][:, :, MLA_NOPE:], (1, 2, 0)).astype(BF)
        bf128 = jnp.zeros((1, LANES), F32).at[0, MLA_ROPE:MLA_ROPE + N_HEADS].set(b_fox_f[l])
        gq, gkv = g_mla_q[l].reshape(1, -1), g_mla_kv[l].reshape(1, -1)
        wb, wo = w_branch[l].astype(BF), w_out[l].astype(BF)
        i = l // 2
        if l % 2 == 0:
            wg, wu, wd = ffn_w_gate[i][None].astype(BF), ffn_w_up[i][None].astype(BF), ffn_w_down[i][None].astype(BF)
            w_rt = jnp.zeros((LANES, D), F32)
            b_rt = jnp.zeros((1, LANES), F32)
        else:
            wg, wu, wd = moe_w_gate[i].astype(BF), moe_w_up[i].astype(BF), moe_w_down[i].astype(BF)
            n_e = wg.shape[0]
            w_rt = jnp.zeros((LANES, D), F32).at[:n_e].set(moe_w_router[i].T)
            b_rt = jnp.zeros((1, LANES), F32).at[0, :n_e].set(moe_b_router[i])
        tf = _ffn_tile(wg.shape[2])
        final = l == L - 1

        z = _inproj(tok_p_in, xp, mod_p, l, norm_mix[l], wt_perm)
        qlat, qrope, ckv, krlf = _prep(tok_p, z, gq, gkv, wn, wr, wuk, bf128, rope_p)
        logf = krlf[:, MLA_ROPE:MLA_ROPE + N_HEADS].reshape(Bp, Tp, N_HEADS)
        cum_t = _cumsum_time(jnp.swapaxes(logf, 1, 2).reshape(Bp * N_HEADS, Tp // LANES, LANES))
        cum_t = cum_t.reshape(Bp, N_HEADS, Tp)
        o_fox = _fox_prompt(z, jnp.swapaxes(cum_t, 1, 2), cum_t, Bp, Tp, tq_p, tk_p)
        o_mla = _mla_prompt(qlat, qrope, ckv, krlf, wuv_t, Bp, Tp, tq_p, tk_p)
        kmean = _kmean_prompt(z, Bp * Tp).reshape(Bp, Tp // blk, BRANCH_W)
        o_moba = _moba_prompt(z, kmean, bias_p, Bp, Tp)
        y_conv, buf_p = _conv(z, hist_p, w_dw[l], b_dw[l], conv_ln_g[l], conv_ln_b[l], Bp, Tp, tok_p.tt)
        xp = _merge(tok_p, xp, mod_p, l, z, o_fox, o_mla, o_moba, y_conv, wb, wo)
        xp = _ffn(tok_p_in, xp, mod_p, l, norm_ffn[l], w_rt, b_rt, wg, wu, wd, norm_out, tf, final)
        heads = lambda a: a.reshape(Bp, Tp, N_HEADS, HEAD_DIM)
        rows_p.append((heads(z[CH_FQK, :, BRANCH_W:]), heads(z[CH_FV_BQ, :, :BRANCH_W]), logf,
                       ckv.reshape(Bp, Tp, MLA_KV_RANK), krlf[:, :MLA_ROPE].reshape(Bp, Tp, MLA_ROPE),
                       heads(z[CH_BKV, :, :BRANCH_W]), heads(z[CH_BKV, :, BRANCH_W:]), buf_p))

        z = _inproj(tok_s, xs, mod_s, l, norm_mix[l], wt_perm)
        qlat, qrope, ckv, krlf = _prep(tok_s, z, gq, gkv, wn, wr, wuk, bf128, rope_s)
        logf = krlf[:, MLA_ROPE:MLA_ROPE + N_HEADS].reshape(Bs, Ts, N_HEADS)
        lf_new_t = jnp.pad(jnp.swapaxes(logf, 1, 2), ((0, 0), (0, 0), (0, LANES - Ts)))
        o_fox = _fox_decode(z, lf_new_t, pk_fox, pv_fox, p_lft, pt_flat, l, Bs, Ts, np_, pg)
        o_mla = _mla_decode(qlat, qrope, ckv, krlf, wuv, cache_mla_ckv, p_krt, pt_flat, l, Bs, Ts, np_, pg)
        o_moba = _moba_decode(z, bias_s, pk_moba, pv_moba, pt_flat, l, Bs, Ts, np_, pg)
        y_conv, buf_s = _conv(z, hist_s[l], w_dw[l], b_dw[l], conv_ln_g[l], conv_ln_b[l], Bs, Ts, Ts)
        xs = _merge(tok_s, xs, mod_s, l, z, o_fox, o_mla, o_moba, y_conv, wb, wo)
        xs = _ffn(tok_s, xs, mod_s, l, norm_ffn[l], w_rt, b_rt, wg, wu, wd, norm_out, tf, final)
        heads = lambda a: a.reshape(Bs, Ts, N_HEADS, HEAD_DIM)
        rows_s.append((heads(z[CH_FQK, :, BRANCH_W:]), heads(z[CH_FV_BQ, :, :BRANCH_W]), logf,
                       ckv.reshape(Bs, Ts, MLA_KV_RANK), krlf[:, :MLA_ROPE].reshape(Bs, Ts, MLA_ROPE),
                       heads(z[CH_BKV, :, :BRANCH_W]), heads(z[CH_BKV, :, BRANCH_W:]), buf_s))

    stack = lambda rows: [jnp.stack([r[j] for r in rows]) for j in range(8)]
    return (xp, xs, *stack(rows_p), *stack(rows_s))
```

```python
import functools
import math

import numpy as np
import jax
import jax.numpy as jnp
from jax import lax
from jax.experimental import pallas as pl
from jax.experimental.pallas import tpu as pltpu

F32 = jnp.float32
BF = jnp.bfloat16

D_MODEL = 1024
N_HEADS = 4
HEAD_DIM = 64
BRANCH_W = 256
MLA_Q_RANK = 256
MLA_KV_RANK = 128
MLA_NOPE = 64
MLA_ROPE = 32
CONV_W = 31
MOBA_BLOCK = 256
MOBA_TOPK = 3
REL_BUCKETS = 32
REL_MAX_DIST = 128
ROPE_THETA = 10000.0
EPS = 1e-6
PAGE = 128
IN_SIZES = (256, 256, 256, 4, 256, 128, 32, 256, 256, 256, 256, 256, 4096)

LANES = 128
CHUNK = 512
N_CHUNKS = 13
CH_FQK, CH_FV_BQ, CH_BKV, CH_MLA, CH_CONV = 8, 9, 10, 11, 12
NEG = -1e30
VMEM_LIMIT = 56 * 2 ** 20
QK_SCALE = HEAD_DIM ** -0.5


def _params(sem):
    return pltpu.CompilerParams(dimension_semantics=sem, vmem_limit_bytes=VMEM_LIMIT)


def _split3(x):
    x1 = x.astype(BF)
    r1 = x - x1.astype(F32)
    x2 = r1.astype(BF)
    x3 = (r1 - x2.astype(F32)).astype(BF)
    return x1, x2, x3


def _dot_exact01(x, m01):
    acc = None
    for piece in _split3(x):
        t = jnp.dot(piece, m01, preferred_element_type=F32)
        acc = t if acc is None else acc + t
    return acc


def _dot01_exact(m01, x):
    acc = None
    for piece in _split3(x):
        t = jnp.dot(m01, piece, preferred_element_type=F32)
        acc = t if acc is None else acc + t
    return acc


def _dot_f32(a, b, dn):
    a1, a2, a3 = _split3(a)
    b1, b2, b3 = _split3(b)
    acc = None
    for x, y in ((a1, b1), (a1, b2), (a2, b1), (a2, b2), (a1, b3), (a3, b1)):
        t = lax.dot_general(x, y, dn, preferred_element_type=F32)
        acc = t if acc is None else acc + t
    return acc


_NT = (((1,), (1,)), ((), ()))
_NN = (((1,), (0,)), ((), ()))


def _dot_nt(a, b):
    return lax.dot_general(a, b, _NT, preferred_element_type=F32)


def _log_sigmoid(x):
    return jnp.minimum(x, 0.0) - jnp.log1p(jnp.exp(-jnp.abs(x)))


def _rel_thresholds():
    max_exact = REL_BUCKETS // 2
    n = np.arange(0, 4 * REL_MAX_DIST)
    nf = np.maximum(n, 1).astype(np.float64)
    large = max_exact + (np.log(nf / max_exact) / math.log(REL_MAX_DIST / max_exact)
                         * (REL_BUCKETS - max_exact)).astype(np.int64)
    bucket = np.where(n < max_exact, n, np.minimum(large, REL_BUCKETS - 1))
    return [int(np.argmax(bucket >= k)) for k in range(1, REL_BUCKETS)]


_REL_THR = _rel_thresholds()


class _Tok:
    def __init__(self, B, T, bb, tt):
        assert B % bb == 0 and T % tt == 0 and (bb == 1 or tt == T) and tt % 8 == 0
        self.B, self.T, self.bb, self.tt = B, T, bb, tt
        self.nt = T // tt
        self.n_tiles = (B // bb) * self.nt
        self.tm = bb * tt
        self.N = B * T

    def x_spec(self):
        nt = self.nt
        return pl.BlockSpec((self.bb, self.tt, D_MODEL), lambda i, *_: (i // nt, i % nt, 0))

    def mod_spec(self, l, k):
        nt = self.nt
        return pl.BlockSpec((None, None, self.bb, 1, D_MODEL), lambda i, *_: (l, k, i // nt, 0, 0))

    def flat_spec(self, width):
        return pl.BlockSpec((self.tm, width), lambda i, *_: (i, 0))

    def z_spec(self, chunk, n=None):
        if n is None:
            return pl.BlockSpec((None, self.tm, CHUNK), lambda i, *_: (chunk, i, 0))
        return pl.BlockSpec((n, self.tm, CHUNK), lambda i, *_: (chunk // n, i, 0))


def _ada_kernel(c_ref, w_ref, b_ref, o_ref):
    c = c_ref[...]
    s = (c * jax.nn.sigmoid(c)).astype(BF)
    o_ref[...] = jnp.dot(s, w_ref[0].astype(BF), preferred_element_type=F32) + b_ref[0]


def _ada(c_all, w_ada, b_ada):
    L, D, D6 = w_ada.shape
    Mc = c_all.shape[0]
    tn = CHUNK
    per = D // tn
    return pl.pallas_call(
        _ada_kernel, grid=(L, D6 // tn),
        in_specs=[pl.BlockSpec((Mc, D), lambda l, j: (0, 0)),
                  pl.BlockSpec((1, D, tn), lambda l, j: (l, 0, j)),
                  pl.BlockSpec((1, 1, tn), lambda l, j: (l, 0, j))],
        out_specs=pl.BlockSpec((None, None, Mc, tn), lambda l, j: (l, j // per, 0, j % per)),
        out_shape=jax.ShapeDtypeStruct((L, D6 // D, Mc, D), F32),
        compiler_params=_params(("arbitrary", "arbitrary")), name="ada",
    )(c_all, w_ada, b_ada.reshape(L, 1, D6))


def _modulated_norm(x, g, sc, sh):
    ms = jnp.mean(x * x, axis=-1, keepdims=True)
    y = x * lax.rsqrt(ms + EPS) * g
    return y * (1.0 + sc) + sh


def _inproj_kernel(x_ref, sh_ref, sc_ref, g_ref, wt_ref, z_ref, h_scr):
    j = pl.program_id(1)

    @pl.when(j == 0)
    def _():
        h = _modulated_norm(x_ref[...], g_ref[...], sc_ref[...], sh_ref[...])
        h_scr[...] = h.reshape(h_scr.shape).astype(BF)

    acc = _dot_nt(h_scr[...], wt_ref[...])

    @pl.when(j < 8)
    def _():
        z_ref[...] = jax.nn.sigmoid(acc)

    @pl.when(j >= 8)
    def _():
        z_ref[...] = acc


def _inproj(tok, x, mod, l, g_norm, wt_perm):
    return pl.pallas_call(
        _inproj_kernel, grid=(tok.n_tiles, N_CHUNKS),
        in_specs=[tok.x_spec(), tok.mod_spec(l, 0), tok.mod_spec(l, 1),
                  pl.BlockSpec((1, D_MODEL), lambda i, j: (0, 0)),
                  pl.BlockSpec((CHUNK, D_MODEL), lambda i, j: (j, 0))],
        out_specs=pl.BlockSpec((None, tok.tm, CHUNK), lambda i, j: (j, i, 0)),
        out_shape=jax.ShapeDtypeStruct((N_CHUNKS, tok.N, CHUNK), F32),
        scratch_shapes=[pltpu.VMEM((tok.tm, D_MODEL), BF)],
        compiler_params=_params(("arbitrary", "arbitrary")), name="inproj",
    )(x, mod, mod, g_norm.reshape(1, D_MODEL), wt_perm)


def _perm_w_in_t(w):
    wt = w.T
    offs = np.concatenate([[0], np.cumsum(IN_SIZES)])
    fq, fk, fv, ff, mcq, mckv, mkr, bq, bk, bv, cu, cg, gate = [wt[offs[i]:offs[i + 1]] for i in range(13)]
    pad = jnp.zeros((LANES - MLA_ROPE - N_HEADS, w.shape[0]), w.dtype)
    return jnp.concatenate([gate, fq, fk, fv, bq, bk, bv, mcq, mckv, mkr, ff, pad, cu, cg], axis=0).astype(BF)


def _rope_lanes(x, c, sa, sb):
    return x * c + pltpu.roll(x, LANES - MLA_ROPE // 2, 1) * sa + pltpu.roll(x, MLA_ROPE // 2, 1) * sb


def _prep_kernel(z_ref, gq_ref, gkv_ref, wn_ref, wr_ref, wuk_ref, bf_ref, c_ref, sa_ref, sb_ref,
                 qlat_ref, qrope_ref, ckv_ref, krlf_ref):
    z = z_ref[...]
    mcq, mckv, kf = z[:, :256], z[:, 256:384], z[:, 384:512]
    qn = (mcq * lax.rsqrt(jnp.mean(mcq * mcq, axis=-1, keepdims=True) + EPS) * gq_ref[...]).astype(BF)
    q_nope = jnp.dot(qn, wn_ref[...], preferred_element_type=F32)
    q_rope = jnp.dot(qn, wr_ref[...], preferred_element_type=F32)
    c, sa, sb = c_ref[...], sa_ref[...], sb_ref[...]
    qrope_ref[...] = _rope_lanes(q_rope, c, sa, sb)
    for h in range(N_HEADS):
        qh = q_nope[:, h * MLA_NOPE:(h + 1) * MLA_NOPE].astype(BF)
        qlat_ref[:, h * MLA_KV_RANK:(h + 1) * MLA_KV_RANK] = jnp.dot(qh, wuk_ref[h], preferred_element_type=F32)
    ckv_ref[...] = mckv * lax.rsqrt(jnp.mean(mckv * mckv, axis=-1, keepdims=True) + EPS) * gkv_ref[...]
    lane = lax.broadcasted_iota(jnp.int32, kf.shape, 1)
    roped = _rope_lanes(kf, c, sa, sb)
    logf = _log_sigmoid(kf + bf_ref[...])
    krlf_ref[...] = jnp.where(lane < MLA_ROPE, roped, jnp.where(lane < MLA_ROPE + N_HEADS, logf, 0.0))


def _prep(tok, z, gq, gkv, wn, wr, wuk, bf128, rope_tabs):
    N = tok.N
    full = lambda a: pl.BlockSpec(a.shape, lambda i: (0,) * a.ndim)
    tab_spec = pl.BlockSpec((tok.tm, LANES), lambda i: (i % max(tok.nt, 1), 0)) if tok.bb == 1 else \
        pl.BlockSpec((tok.tm, LANES), lambda i: (0, 0))
    c, sa, sb = rope_tabs
    return pl.pallas_call(
        _prep_kernel, grid=(tok.n_tiles,),
        in_specs=[tok.z_spec(CH_MLA), full(gq), full(gkv), full(wn), full(wr), full(wuk), full(bf128),
                  tab_spec, tab_spec, tab_spec],
        out_specs=[tok.flat_spec(512), tok.flat_spec(128), tok.flat_spec(128), tok.flat_spec(128)],
        out_shape=[jax.ShapeDtypeStruct((N, 512), F32), jax.ShapeDtypeStruct((N, 128), F32),
                   jax.ShapeDtypeStruct((N, 128), F32), jax.ShapeDtypeStruct((N, 128), F32)],
        compiler_params=_params(("arbitrary",)), name="mla_prep",
    )(z, gq, gkv, wn, wr, wuk, bf128, c, sa, sb)


def _rope_tables(pos, reps):
    half = MLA_ROPE // 2
    inv = ROPE_THETA ** (-jnp.arange(half, dtype=F32) / half)
    ang = pos.astype(F32)[:, None] * inv[None, :]
    cos, sin = jnp.cos(ang), jnp.sin(ang)
    zero = jnp.zeros_like(sin)
    tile = lambda a, b: jnp.tile(jnp.concatenate([a, b], axis=1), (reps, LANES // MLA_ROPE))
    return tile(cos, cos), tile(-sin, zero), tile(zero, sin)


def _upper_tri():
    r = lax.broadcasted_iota(jnp.int32, (LANES, LANES), 0)
    c = lax.broadcasted_iota(jnp.int32, (LANES, LANES), 1)
    return (r <= c).astype(BF)


def _cumsum_kernel(x_ref, o_ref):
    x = x_ref[0]
    R = x.shape[0]
    within = _dot_exact01(x, _upper_tri())
    tot = jnp.broadcast_to(within[:, LANES - 1:LANES], (R, LANES))
    rr = lax.broadcasted_iota(jnp.int32, (R, R), 0)
    cc = lax.broadcasted_iota(jnp.int32, (R, R), 1)
    o_ref[0] = within + _dot01_exact((cc < rr).astype(BF), tot)


def _cumsum_time(x):
    G, R, _ = x.shape
    return pl.pallas_call(
        _cumsum_kernel, grid=(G,),
        in_specs=[pl.BlockSpec((1, R, LANES), lambda g: (g, 0, 0))],
        out_specs=pl.BlockSpec((1, R, LANES), lambda g: (g, 0, 0)),
        out_shape=jax.ShapeDtypeStruct(x.shape, F32),
        compiler_params=_params(("arbitrary",)), name="fox_cumsum",
    )(x)


def _init_stats(m_scr, l_scr, acc_scr):
    m_scr[...] = jnp.full(m_scr.shape, NEG, F32)
    l_scr[...] = jnp.zeros(l_scr.shape, F32)
    acc_scr[...] = jnp.zeros(acc_scr.shape, F32)


def _online_t(st, vt_bf, m_ref, l_ref, acc_ref, col_on=None, shift=None):
    m_prev = m_ref[...]
    m_tile = jnp.max(st, axis=0, keepdims=True)
    if shift is not None:
        m_tile = m_tile + shift
    m_new = jnp.maximum(m_prev, m_tile)
    if col_on is not None:
        m_new = jnp.where(col_on, m_new, m_prev)
    alpha = jnp.exp(m_prev - m_new)
    m_use = m_new if shift is None else m_new - shift
    if col_on is not None:
        m_use = jnp.where(col_on, m_use, -NEG)
    p = jnp.exp(st - m_use)
    l_ref[...] = alpha * l_ref[...] + jnp.sum(p, axis=0, keepdims=True)
    m_ref[...] = m_new
    acc_ref[...] = alpha * acc_ref[...] + jnp.dot(vt_bf, p.astype(BF), preferred_element_type=F32)


def _strip_plan(d, tk, tq, sw):
    for s in range(tq // sw):
        ss = slice(s * sw, (s + 1) * sw)
        if d is None:
            yield ss, None
            continue
        delta = s * sw - d * tk
        if delta <= -sw:
            continue
        yield ss, (None if delta >= tk - 1 else delta)


def _strip_width(tq):
    return tq


def _visible(tk, sw, delta):
    row = lax.broadcasted_iota(jnp.int32, (tk, sw), 0)
    col = lax.broadcasted_iota(jnp.int32, (tk, sw), 1)
    return row <= col + delta


def _attn_scratch(tq, dv):
    return [pltpu.VMEM((8, tq), F32), pltpu.VMEM((8, tq), F32), pltpu.VMEM((N_HEADS * dv, tq), F32)]


def _causal_branches(qi, ki, ratio, process):
    d = ki - ratio * qi

    @pl.when(d < 0)
    def _():
        process(None)

    for dd in range(ratio):
        @pl.when(d == dd)
        def _(dd=dd):
            process(dd)


def _fox_kernel(q_ref, k_ref, v_ref, ck_ref, cq_ref, o_ref, m_scr, l_scr, acc_scr, *, tq, tk, sw):
    qi, ki = pl.program_id(1), pl.program_id(2)

    @pl.when(ki == 0)
    def _():
        _init_stats(m_scr, l_scr, acc_scr)

    def process(d):
        q = (q_ref[:, :BRANCH_W] * QK_SCALE).astype(BF)
        k = k_ref[:, BRANCH_W:].astype(BF)
        vt = v_ref[:, :BRANCH_W].T.astype(BF)
        ck = ck_ref[0]
        cq = cq_ref[0]
        for h in range(N_HEADS):
            hs = slice(h * HEAD_DIM, (h + 1) * HEAD_DIM)
            for ss, delta in _strip_plan(d, tk, tq, sw):
                st = _dot_nt(k[:, hs], q[ss, hs]) + cq[h:h + 1, ss] - ck[:, h:h + 1]
                if delta is not None:
                    st = jnp.where(_visible(tk, sw, delta), st, NEG)
                _online_t(st, vt[hs, :], m_scr.at[h:h + 1, ss], l_scr.at[h:h + 1, ss], acc_scr.at[hs, ss])

    _causal_branches(qi, ki, tq // tk, process)

    @pl.when(ki == pl.num_programs(2) - 1)
    def _():
        ot = jnp.concatenate([acc_scr[h * HEAD_DIM:(h + 1) * HEAD_DIM, :] / l_scr[h:h + 1, :]
                              for h in range(N_HEADS)], axis=0)
        o_ref[...] = ot.T


def _fox_prompt(z, cum, cum_t, B, T, tq, tk):
    nq, nk, ratio = T // tq, T // tk, tq // tk
    last_k = lambda qi, ki: jnp.minimum(ki, ratio * qi + ratio - 1)
    zq = lambda ch: pl.BlockSpec((None, tq, CHUNK), lambda b, qi, ki: (ch, b * nq + qi, 0))
    zk = lambda ch: pl.BlockSpec((None, tk, CHUNK), lambda b, qi, ki: (ch, b * nk + last_k(qi, ki), 0))
    return pl.pallas_call(
        functools.partial(_fox_kernel, tq=tq, tk=tk, sw=_strip_width(tq)), grid=(B, nq, nk),
        in_specs=[zq(CH_FQK), zk(CH_FQK), zk(CH_FV_BQ),
                  pl.BlockSpec((1, tk, N_HEADS), lambda b, qi, ki: (b, last_k(qi, ki), 0)),
                  pl.BlockSpec((1, N_HEADS, tq), lambda b, qi, ki: (b, 0, qi))],
        out_specs=pl.BlockSpec((tq, BRANCH_W), lambda b, qi, ki: (b * nq + qi, 0)),
        out_shape=jax.ShapeDtypeStruct((B * T, BRANCH_W), F32),
        scratch_shapes=_attn_scratch(tq, HEAD_DIM),
        compiler_params=_params(("arbitrary", "arbitrary", "arbitrary")), name="fox_prompt",
    )(z, z, z, cum, cum_t)


_MLA_SCALE = (MLA_NOPE + MLA_ROPE) ** -0.5


def _mla_kernel(ql_ref, qr_ref, ckv_ref, kr_ref, wuvt_ref, o_ref, m_scr, l_scr, acc_scr, *, tq, tk, sw):
    qi, ki = pl.program_id(1), pl.program_id(2)

    @pl.when(ki == 0)
    def _():
        _init_stats(m_scr, l_scr, acc_scr)

    def process(d):
        ckv = ckv_ref[...]
        c_bf = ckv.astype(BF)
        ct_bf = ckv.T.astype(BF)
        kr = kr_ref[:, :MLA_ROPE].astype(BF)
        ql = ql_ref[...].astype(BF)
        qr = qr_ref[...].astype(BF)
        for h in range(N_HEADS):
            ls = slice(h * MLA_KV_RANK, (h + 1) * MLA_KV_RANK)
            rs = slice(h * MLA_ROPE, (h + 1) * MLA_ROPE)
            for ss, delta in _strip_plan(d, tk, tq, sw):
                st = (_dot_nt(c_bf, ql[ss, ls]) + _dot_nt(kr, qr[ss, rs])) * _MLA_SCALE
                if delta is not None:
                    st = jnp.where(_visible(tk, sw, delta), st, NEG)
                _online_t(st, ct_bf, m_scr.at[h:h + 1, ss], l_scr.at[h:h + 1, ss], acc_scr.at[ls, ss])

    _causal_branches(qi, ki, tq // tk, process)

    @pl.when(ki == pl.num_programs(2) - 1)
    def _():
        parts = []
        for h in range(N_HEADS):
            ls = slice(h * MLA_KV_RANK, (h + 1) * MLA_KV_RANK)
            o_lat_t = (acc_scr[ls, :] / l_scr[h:h + 1, :]).astype(BF)
            parts.append(jnp.dot(wuvt_ref[h], o_lat_t, preferred_element_type=F32))
        o_ref[...] = jnp.concatenate(parts, axis=0).T


def _mla_prompt(qlat, qrope, ckv, krlf, wuv_t, B, T, tq, tk):
    nq, nk, ratio = T // tq, T // tk, tq // tk
    qs = lambda w: pl.BlockSpec((tq, w), lambda b, qi, ki: (b * nq + qi, 0))
    ks = lambda w: pl.BlockSpec((tk, w), lambda b, qi, ki: (b * nk + jnp.minimum(ki, ratio * qi + ratio - 1), 0))
    return pl.pallas_call(
        functools.partial(_mla_kernel, tq=tq, tk=tk, sw=_strip_width(tq)), grid=(B, nq, nk),
        in_specs=[qs(512), qs(128), ks(128), ks(128),
                  pl.BlockSpec(wuv_t.shape, lambda b, qi, ki: (0, 0, 0))],
        out_specs=pl.BlockSpec((tq, BRANCH_W), lambda b, qi, ki: (b * nq + qi, 0)),
        out_shape=jax.ShapeDtypeStruct((B * T, BRANCH_W), F32),
        scratch_shapes=_attn_scratch(tq, MLA_KV_RANK),
        compiler_params=_params(("arbitrary", "arbitrary", "arbitrary")), name="mla_prompt",
    )(qlat, qrope, ckv, krlf, wuv_t)


def _relbias_kernel(tab_ref, qpos_ref, kpos_ref, o_ref):
    dist = jnp.maximum(qpos_ref[...] - kpos_ref[...], 0)
    tab = tab_ref[...]
    out = jnp.broadcast_to(tab[:, 0:1], dist.shape)
    for k in range(1, REL_BUCKETS):
        out = jnp.where(dist >= _REL_THR[k - 1], tab[:, k:k + 1], out)
    o_ref[...] = out


def _relbias(tab_rows, qpos, kpos):
    R, C = tab_rows.shape[0], kpos.shape[1]
    rt = min(R, 512)
    return pl.pallas_call(
        _relbias_kernel, grid=(R // rt,),
        in_specs=[pl.BlockSpec((rt, REL_BUCKETS), lambda i: (i, 0)),
                  pl.BlockSpec((rt, 1), lambda i: (i, 0)),
                  pl.BlockSpec((1, C), lambda i: (0, 0))],
        out_specs=pl.BlockSpec((rt, C), lambda i: (i, 0)),
        out_shape=jax.ShapeDtypeStruct((R, C), F32),
        compiler_params=_params(("arbitrary",)), name="rel_bias",
    )(tab_rows, qpos, kpos)


def _kmean_kernel(z_ref, o_ref):
    o_ref[0] = jnp.mean(z_ref[:, :BRANCH_W], axis=0, keepdims=True)


def _kmean_prompt(z, N):
    nb = N // MOBA_BLOCK
    return pl.pallas_call(
        _kmean_kernel, grid=(nb,),
        in_specs=[pl.BlockSpec((None, MOBA_BLOCK, CHUNK), lambda i: (CH_BKV, i, 0))],
        out_specs=pl.BlockSpec((1, 1, BRANCH_W), lambda i: (i, 0, 0)),
        out_shape=jax.ShapeDtypeStruct((nb, 1, BRANCH_W), F32),
        compiler_params=_params(("arbitrary",)), name="moba_kmean",
    )(z)


def _top_blocks(gate, n_valid, n_blocks):
    lane = lax.broadcasted_iota(jnp.int32, gate.shape, 1)
    rank = jnp.zeros(gate.shape, F32)
    for m in range(n_blocks):
        gm = gate[:, m:m + 1]
        ahead = jnp.where(gm > gate, 1.0, jnp.where(gm == gate, jnp.where(lane > m, 1.0, 0.0), 0.0))
        rank = rank + ahead * jnp.where(m < n_valid, 1.0, 0.0)
    return jnp.where(lane < n_valid, jnp.where(rank < MOBA_TOPK, 1.0, 0.0), 0.0)


def _top_blocks_t(gate_t, n_valid, n_blocks):
    row = lax.broadcasted_iota(jnp.int32, gate_t.shape, 0)
    rank = jnp.zeros(gate_t.shape, F32)
    for m in range(n_blocks):
        gm = gate_t[m:m + 1, :]
        ahead = jnp.where(gm > gate_t, 1.0, jnp.where(gm == gate_t, jnp.where(row > m, 1.0, 0.0), 0.0))
        rank = rank + ahead * jnp.where(m < n_valid, 1.0, 0.0)
    return jnp.where(row < n_valid, jnp.where(rank < MOBA_TOPK, 1.0, 0.0), 0.0)


def _moba_kernel(q_ref, kv_ref, km_ref, bias_ref, o_ref, m_scr, l_scr, acc_scr, sel_scr, *, nb, sw):
    qi, ki = pl.program_id(1), pl.program_id(2)
    tq = tk = MOBA_BLOCK

    @pl.when(ki == 0)
    def _():
        _init_stats(m_scr, l_scr, acc_scr)
        q = q_ref[:, BRANCH_W:]
        km = km_ref[0]
        for h in range(N_HEADS):
            hs = slice(h * HEAD_DIM, (h + 1) * HEAD_DIM)
            g_t = _dot_f32(km[:, hs], q[:, hs], _NT)
            sel_t = _top_blocks_t(g_t, qi, nb)
            for n in range(nb):
                sel_scr[h, n] = sel_t[n:n + 1, :]

    def process(kind):
        q = (q_ref[:, BRANCH_W:] * QK_SCALE).astype(BF)
        k = kv_ref[:, :BRANCH_W].astype(BF)
        vt = kv_ref[:, BRANCH_W:].T.astype(BF)
        for h in range(N_HEADS):
            hs = slice(h * HEAD_DIM, (h + 1) * HEAD_DIM)
            for s in range(tq // sw):
                ss = slice(s * sw, (s + 1) * sw)
                st = _dot_nt(k[:, hs], q[ss, hs])
                stats = (m_scr.at[h:h + 1, ss], l_scr.at[h:h + 1, ss], acc_scr.at[hs, ss])
                if kind == "diag":
                    st = jnp.where(_visible(tk, sw, s * sw), st + bias_ref[0, h, :, ss], NEG)
                    _online_t(st, vt[hs, :], *stats)
                    continue
                picked = sel_scr[h, ki][:, ss] > 0.0
                if kind == "near":
                    _online_t(st + bias_ref[0, h, :, ss], vt[hs, :], *stats, col_on=picked)
                else:
                    _online_t(st, vt[hs, :], *stats, col_on=picked, shift=bias_ref[0, h, 0:1, 0:1])

    @pl.when(ki < qi - 1)
    def _():
        process("far")

    @pl.when(ki == qi - 1)
    def _():
        process("near")

    @pl.when(ki == qi)
    def _():
        process("diag")

    @pl.when(ki == pl.num_programs(2) - 1)
    def _():
        ot = jnp.concatenate([acc_scr[h * HEAD_DIM:(h + 1) * HEAD_DIM, :] / l_scr[h:h + 1, :]
                              for h in range(N_HEADS)], axis=0)
        o_ref[...] = ot.T


def _moba_prompt(z, kmean, bias, B, T):
    tq = MOBA_BLOCK
    nq = T // tq
    assert nq <= LANES
    return pl.pallas_call(
        functools.partial(_moba_kernel, nb=nq, sw=_strip_width(tq)), grid=(B, nq, nq),
        in_specs=[pl.BlockSpec((None, tq, CHUNK), lambda b, qi, ki: (CH_FV_BQ, b * nq + qi, 0)),
                  pl.BlockSpec((None, tq, CHUNK), lambda b, qi, ki: (CH_BKV, b * nq + jnp.minimum(ki, qi), 0)),
                  pl.BlockSpec((1, nq, BRANCH_W), lambda b, qi, ki: (b, 0, 0)),
                  pl.BlockSpec((1, N_HEADS, tq, tq),
                               lambda b, qi, ki: (jnp.clip(qi - ki, 0, 2), 0, 0, 0))],
        out_specs=pl.BlockSpec((tq, BRANCH_W), lambda b, qi, ki: (b * nq + qi, 0)),
        out_shape=jax.ShapeDtypeStruct((B * T, BRANCH_W), F32),
        scratch_shapes=_attn_scratch(tq, HEAD_DIM) + [pltpu.VMEM((N_HEADS, nq, 1, tq), F32)],
        compiler_params=_params(("arbitrary", "arbitrary", "arbitrary")), name="moba_prompt",
    )(z, z, kmean, bias)


def _head_rows(x, width):
    return jnp.concatenate([x[:, h * width:(h + 1) * width] for h in range(N_HEADS)], axis=0)


def _blockdiag_rows(q):
    lane_head = lax.broadcasted_iota(jnp.int32, q.shape, 1) // HEAD_DIM
    return jnp.concatenate([jnp.where(lane_head == h, q, 0.0) for h in range(N_HEADS)], axis=0)


def _diag_heads(o_all, tq):
    lane_head = lax.broadcasted_iota(jnp.int32, (tq, o_all.shape[1]), 1) // HEAD_DIM
    out = jnp.zeros((tq, o_all.shape[1]), F32)
    for h in range(N_HEADS):
        out = out + jnp.where(lane_head == h, o_all[h * tq:(h + 1) * tq], 0.0)
    return out


def _rows_per_head(x, tq):
    return jnp.concatenate([jnp.broadcast_to(x[h:h + 1], (tq, x.shape[1])) for h in range(N_HEADS)], axis=0)


SEQS_PER_STEP = 2


def _page_specs(block, l, pg, np_):
    zeros = (0,) * (len(block) - 2)
    ns = SEQS_PER_STEP
    return [pl.BlockSpec(block, lambda b, s, pt, j=j, p=p: (l, pt[(ns * b + j) * np_ + s * pg + p]) + zeros)
            for j in range(ns) for p in range(pg)]


_KT_BLOCK = (None, None, N_HEADS, HEAD_DIM, PAGE)


def _kt(ref):
    return ref[...].reshape(BRANCH_W, PAGE).astype(BF)


def _new_rows_mask(R, tq):
    row_t = lax.broadcasted_iota(jnp.int32, (R, LANES), 0) % tq
    col = lax.broadcasted_iota(jnp.int32, (R, LANES), 1)
    return col <= row_t


def _tile_update(s_all, pv_fn, m_scr, l_scr, acc_scr):
    m_prev = m_scr[...][:, :1]
    m_new = jnp.maximum(m_prev, jnp.max(s_all, axis=-1, keepdims=True))
    alpha = jnp.exp(m_prev - m_new)
    p = jnp.exp(s_all - m_new)
    l_new = alpha * l_scr[...][:, :1] + jnp.sum(p, axis=-1, keepdims=True)
    l_scr[...] = jnp.broadcast_to(l_new, l_scr.shape)
    m_scr[...] = jnp.broadcast_to(m_new, m_scr.shape)
    acc_scr[...] = alpha * acc_scr[...] + pv_fn(p.astype(BF))


def _fox_dec_kernel(pt_ref, zq_ref, zv_ref, lfn_ref, *rest, pg, tq):
    ns = SEQS_PER_STEP
    n_pg = ns * pg
    k_refs, v_refs, lf_refs = rest[:n_pg], rest[n_pg:2 * n_pg], rest[2 * n_pg:3 * n_pg]
    o_ref = rest[3 * n_pg]
    qbd, m_scr, l_scr, acc_scr, run_scr, padk, padv = rest[3 * n_pg + 1:]
    s_idx = pl.program_id(1)
    n_steps = pl.num_programs(1)
    R = N_HEADS * tq
    G = pg * N_HEADS

    @pl.when(s_idx == 0)
    def _():
        for j in range(ns):
            _init_stats(m_scr.at[j], l_scr.at[j], acc_scr.at[j])
            qbd[j] = _blockdiag_rows(zq_ref[j * tq:(j + 1) * tq, :BRANCH_W] * QK_SCALE).astype(BF)
            run_scr[j] = jnp.zeros((G, LANES), F32)

    def tile(j, with_new):
        kj, vj, lfj = (r[j * pg:(j + 1) * pg] for r in (k_refs, v_refs, lf_refs))
        upper = _upper_tri()
        lf = jnp.concatenate([lfj[p][...] for p in range(pg)], axis=0)
        within = _dot_exact01(lf, upper)
        tot = jnp.broadcast_to(within[:, LANES - 1:LANES], (G, LANES))
        r = lax.broadcasted_iota(jnp.int32, (G, G), 0)
        c = lax.broadcasted_iota(jnp.int32, (G, G), 1)
        same_head = (r % N_HEADS) == (c % N_HEADS)
        earlier = jnp.where(same_head, jnp.where(c // N_HEADS < r // N_HEADS, 1.0, 0.0), 0.0).astype(BF)
        run = run_scr[j]
        fk = run + _dot01_exact(earlier, tot) + within
        run_new = run + _dot01_exact(jnp.where(same_head, 1.0, 0.0).astype(BF), tot)
        run_scr[j] = run_new
        q = qbd[j]
        parts = [jnp.dot(q, _kt(kj[p]), preferred_element_type=F32)
                 - _rows_per_head(fk[p * N_HEADS:(p + 1) * N_HEADS], tq) for p in range(pg)]
        if with_new:
            lfn8 = jnp.concatenate([lfn_ref[j], jnp.zeros((8 - N_HEADS, LANES), F32)], axis=0)
            fkn = run_new[0:8] + _dot_exact01(lfn8, upper)
            s_new = _dot_nt(q, padk[j].astype(BF)) - _rows_per_head(fkn, tq)
            parts.append(jnp.where(_new_rows_mask(R, tq), s_new, NEG))

        def pv_fn(p):
            out = None
            for i in range(pg):
                t = _dot_nt(p[:, i * PAGE:(i + 1) * PAGE], _kt(vj[i]))
                out = t if out is None else out + t
            if with_new:
                out = out + jnp.dot(p[:, pg * PAGE:], padv[j].astype(BF), preferred_element_type=F32)
            return out

        _tile_update(jnp.concatenate(parts, axis=1), pv_fn, m_scr.at[j], l_scr.at[j], acc_scr.at[j])

    @pl.when(s_idx < n_steps - 1)
    def _():
        for j in range(ns):
            tile(j, False)

    @pl.when(s_idx == n_steps - 1)
    def _():
        padk[...] = jnp.zeros(padk.shape, F32)
        padv[...] = jnp.zeros(padv.shape, F32)
        for j in range(ns):
            padk[j, 0:tq, :] = zq_ref[j * tq:(j + 1) * tq, BRANCH_W:]
            padv[j, 0:tq, :] = zv_ref[j * tq:(j + 1) * tq, :BRANCH_W]
        for j in range(ns):
            tile(j, True)
        for j in range(ns):
            o_ref[j * tq:(j + 1) * tq, :] = _diag_heads(acc_scr[j] / l_scr[j][:, :1], tq)


def _fox_decode(z, lf_new_t, pool_kt, pool_vt, pool_lft, pt_flat, l, B, tq, np_, pg):
    R = N_HEADS * tq
    ns = SEQS_PER_STEP
    grid_spec = pltpu.PrefetchScalarGridSpec(
        num_scalar_prefetch=1, grid=(B // ns, np_ // pg),
        in_specs=[pl.BlockSpec((None, ns * tq, CHUNK), lambda b, s, pt: (CH_FQK, b, 0)),
                  pl.BlockSpec((None, ns * tq, CHUNK), lambda b, s, pt: (CH_FV_BQ, b, 0)),
                  pl.BlockSpec((ns, N_HEADS, LANES), lambda b, s, pt: (b, 0, 0))]
        + _page_specs(_KT_BLOCK, l, pg, np_) + _page_specs(_KT_BLOCK, l, pg, np_)
        + _page_specs((None, None, N_HEADS, PAGE), l, pg, np_),
        out_specs=pl.BlockSpec((ns * tq, BRANCH_W), lambda b, s, pt: (b, 0)),
        scratch_shapes=[pltpu.VMEM((ns, R, BRANCH_W), BF), pltpu.VMEM((ns, R, LANES), F32),
                        pltpu.VMEM((ns, R, LANES), F32), pltpu.VMEM((ns, R, BRANCH_W), F32),
                        pltpu.VMEM((ns, pg * N_HEADS, LANES), F32),
                        pltpu.VMEM((ns, PAGE, BRANCH_W), F32), pltpu.VMEM((ns, PAGE, BRANCH_W), F32)])
    n_pg = ns * pg
    return pl.pallas_call(
        functools.partial(_fox_dec_kernel, pg=pg, tq=tq), grid_spec=grid_spec,
        out_shape=jax.ShapeDtypeStruct((B * tq, BRANCH_W), F32),
        compiler_params=_params(("arbitrary", "arbitrary")), name="fox_decode",
    )(pt_flat, z, z, lf_new_t, *([pool_kt] * n_pg), *([pool_vt] * n_pg), *([pool_lft] * n_pg))


def _mla_dec_kernel(pt_ref, ql_ref, qr_ref, cn_ref, kn_ref, wuv_ref, *rest, pg, tq):
    ns = SEQS_PER_STEP
    n_pg = ns * pg
    c_refs, r_refs = rest[:n_pg], rest[n_pg:2 * n_pg]
    o_ref = rest[2 * n_pg]
    ql_scr, qr_scr, m_scr, l_scr, acc_scr, padc, padr = rest[2 * n_pg + 1:]
    s_idx = pl.program_id(1)
    n_steps = pl.num_programs(1)
    R = N_HEADS * tq

    @pl.when(s_idx == 0)
    def _():
        for j in range(ns):
            _init_stats(m_scr.at[j], l_scr.at[j], acc_scr.at[j])
            ql_scr[j] = _head_rows(ql_ref[j * tq:(j + 1) * tq, :], MLA_KV_RANK).astype(BF)
            qr_scr[j] = _head_rows(qr_ref[j * tq:(j + 1) * tq, :], MLA_ROPE).astype(BF)

    def tile(j, with_new):
        cj, rj = c_refs[j * pg:(j + 1) * pg], r_refs[j * pg:(j + 1) * pg]
        ql, qr = ql_scr[j], qr_scr[j]
        cs = [cj[p][...].astype(BF) for p in range(pg)]
        parts = [(_dot_nt(ql, cs[p]) + jnp.dot(qr, rj[p][...].astype(BF), preferred_element_type=F32))
                 * _MLA_SCALE for p in range(pg)]
        if with_new:
            s_new = (_dot_nt(ql, padc[j].astype(BF)) + _dot_nt(qr, padr[j].astype(BF))) * _MLA_SCALE
            parts.append(jnp.where(_new_rows_mask(R, tq), s_new, NEG))

        def pv_fn(p):
            out = None
            for i in range(pg):
                t = jnp.dot(p[:, i * PAGE:(i + 1) * PAGE], cs[i], preferred_element_type=F32)
                out = t if out is None else out + t
            if with_new:
                out = out + jnp.dot(p[:, pg * PAGE:], padc[j].astype(BF), preferred_element_type=F32)
            return out

        _tile_update(jnp.concatenate(parts, axis=1), pv_fn, m_scr.at[j], l_scr.at[j], acc_scr.at[j])

    @pl.when(s_idx < n_steps - 1)
    def _():
        for j in range(ns):
            tile(j, False)

    @pl.when(s_idx == n_steps - 1)
    def _():
        padc[...] = jnp.zeros(padc.shape, F32)
        padr[...] = jnp.zeros(padr.shape, F32)
        for j in range(ns):
            padc[j, 0:tq, :] = cn_ref[j * tq:(j + 1) * tq, :]
            padr[j, 0:tq, :] = kn_ref[j * tq:(j + 1) * tq, :MLA_ROPE]
        for j in range(ns):
            tile(j, True)
        for j in range(ns):
            o_lat = (acc_scr[j] / l_scr[j][:, :1]).astype(BF)
            for h in range(N_HEADS):
                o_ref[j * tq:(j + 1) * tq, h * HEAD_DIM:(h + 1) * HEAD_DIM] = jnp.dot(
                    o_lat[h * tq:(h + 1) * tq], wuv_ref[h], preferred_element_type=F32)


def _mla_decode(qlat, qrope, ckv, krlf, wuv, pool_c, pool_rt, pt_flat, l, B, tq, np_, pg):
    R = N_HEADS * tq
    ns = SEQS_PER_STEP
    row = lambda w: pl.BlockSpec((ns * tq, w), lambda b, s, pt: (b, 0))
    grid_spec = pltpu.PrefetchScalarGridSpec(
        num_scalar_prefetch=1, grid=(B // ns, np_ // pg),
        in_specs=[row(512), row(128), row(128), row(128),
                  pl.BlockSpec(wuv.shape, lambda b, s, pt: (0, 0, 0))]
        + _page_specs((None, None, PAGE, MLA_KV_RANK), l, pg, np_)
        + _page_specs((None, None, MLA_ROPE, PAGE), l, pg, np_),
        out_specs=pl.BlockSpec((ns * tq, BRANCH_W), lambda b, s, pt: (b, 0)),
        scratch_shapes=[pltpu.VMEM((ns, R, MLA_KV_RANK), BF), pltpu.VMEM((ns, R, MLA_ROPE), BF),
                        pltpu.VMEM((ns, R, LANES), F32), pltpu.VMEM((ns, R, LANES), F32),
                        pltpu.VMEM((ns, R, MLA_KV_RANK), F32),
                        pltpu.VMEM((ns, PAGE, MLA_KV_RANK), F32), pltpu.VMEM((ns, PAGE, MLA_ROPE), F32)])
    n_pg = ns * pg
    return pl.pallas_call(
        functools.partial(_mla_dec_kernel, pg=pg, tq=tq), grid_spec=grid_spec,
        out_shape=jax.ShapeDtypeStruct((B * tq, BRANCH_W), F32),
        compiler_params=_params(("arbitrary", "arbitrary")), name="mla_decode",
    )(pt_flat, qlat, qrope, ckv, krlf, wuv, *([pool_c] * n_pg), *([pool_rt] * n_pg))


def _moba_dec_kernel(pt_ref, zq_ref, zkv_ref, bias_ref, *rest, pg, tq, nb):
    ns = SEQS_PER_STEP
    n_pg = ns * pg
    k_refs, v_refs = rest[:n_pg], rest[n_pg:2 * n_pg]
    o_ref = rest[2 * n_pg]
    qbd, qf_scr, km_scr, mb_scr, lb_scr, ob_scr, padk, padv = rest[2 * n_pg + 1:]
    s_idx = pl.program_id(1)
    R = N_HEADS * tq

    @pl.when(s_idx == 0)
    def _():
        for j in range(ns):
            q = _blockdiag_rows(zq_ref[j * tq:(j + 1) * tq, BRANCH_W:] * QK_SCALE)
            qf_scr[j] = q
            qbd[j] = q.astype(BF)
        km_scr[...] = jnp.zeros(km_scr.shape, F32)
        mb_scr[...] = jnp.zeros(mb_scr.shape, F32)
        lb_scr[...] = jnp.zeros(lb_scr.shape, F32)

    def block_stats(s_parts, pv_fn):
        m = s_parts[0].max(axis=-1, keepdims=True)
        for s in s_parts[1:]:
            m = jnp.maximum(m, s.max(axis=-1, keepdims=True))
        ps = [jnp.exp(s - m) for s in s_parts]
        lsum = ps[0].sum(axis=-1, keepdims=True)
        for p in ps[1:]:
            lsum = lsum + p.sum(axis=-1, keepdims=True)
        return m, lsum, pv_fn([p.astype(BF) for p in ps])

    b_far = bias_ref[2 * R:3 * R, :]
    b_last = bias_ref[R:2 * R, :]
    lane = lax.broadcasted_iota(jnp.int32, (BRANCH_W, LANES), 1)
    lane_r = lax.broadcasted_iota(jnp.int32, (R, LANES), 1)
    for i in range(pg // 2):
        n = s_idx * (pg // 2) + i
        bias = jnp.where(n == nb - 1, b_last, b_far)
        for j in range(ns):
            ka, kb = k_refs[j * pg + 2 * i], k_refs[j * pg + 2 * i + 1]
            va, vb = v_refs[j * pg + 2 * i], v_refs[j * pg + 2 * i + 1]
            kta, ktb = ka[...].reshape(BRANCH_W, PAGE), kb[...].reshape(BRANCH_W, PAGE)
            kmean = jnp.sum(kta + ktb, axis=-1, keepdims=True) * (1.0 / MOBA_BLOCK)
            km_scr[j] = jnp.where(lane == n, kmean, km_scr[j])
            q = qbd[j]
            s_a = jnp.dot(q, kta.astype(BF), preferred_element_type=F32) + bias[:, :PAGE]
            s_b = jnp.dot(q, ktb.astype(BF), preferred_element_type=F32) + bias[:, PAGE:]
            m, lsum, o = block_stats(
                [s_a, s_b], lambda ps, va=va, vb=vb: _dot_nt(ps[0], _kt(va)) + _dot_nt(ps[1], _kt(vb)))
            mb_scr[j] = jnp.where(lane_r == n, m, mb_scr[j])
            lb_scr[j] = jnp.where(lane_r == n, lsum, lb_scr[j])
            ob_scr[j, n] = o

    @pl.when(s_idx == pl.num_programs(1) - 1)
    def _():
        padk[...] = jnp.zeros(padk.shape, F32)
        padv[...] = jnp.zeros(padv.shape, F32)
        for j in range(ns):
            padk[j, 0:tq, :] = zkv_ref[j * tq:(j + 1) * tq, :BRANCH_W]
            padv[j, 0:tq, :] = zkv_ref[j * tq:(j + 1) * tq, BRANCH_W:]
        for j in range(ns):
            s_own = jnp.where(_new_rows_mask(R, tq),
                              _dot_nt(qbd[j], padk[j].astype(BF)) + bias_ref[0:R, :PAGE], NEG)
            m_own, l_own, o_own = block_stats(
                [s_own], lambda ps, j=j: jnp.dot(ps[0], padv[j].astype(BF), preferred_element_type=F32))
            g = _dot_f32(qf_scr[j], km_scr[j], _NN)
            on = _top_blocks(g, nb, nb) > 0.0
            m_blk = mb_scr[j]
            m_tot = jnp.maximum(m_own, jnp.max(jnp.where(on, m_blk, NEG), axis=-1, keepdims=True))
            w = jnp.where(on, jnp.exp(m_blk - m_tot), 0.0)
            a_own = jnp.exp(m_own - m_tot)
            l_tot = a_own * l_own + jnp.sum(w * lb_scr[j], axis=-1, keepdims=True)
            o_tot = a_own * o_own
            for n in range(nb):
                o_tot = o_tot + w[:, n:n + 1] * ob_scr[j, n]
            o_ref[j * tq:(j + 1) * tq, :] = _diag_heads(o_tot / l_tot, tq)


def _moba_decode(z, bias, pool_kt, pool_vt, pt_flat, l, B, tq, np_, pg):
    R = N_HEADS * tq
    nb = np_ * PAGE // MOBA_BLOCK
    assert pg % 2 == 0 and nb <= LANES
    ns = SEQS_PER_STEP
    grid_spec = pltpu.PrefetchScalarGridSpec(
        num_scalar_prefetch=1, grid=(B // ns, np_ // pg),
        in_specs=[pl.BlockSpec((None, ns * tq, CHUNK), lambda b, s, pt: (CH_FV_BQ, b, 0)),
                  pl.BlockSpec((None, ns * tq, CHUNK), lambda b, s, pt: (CH_BKV, b, 0)),
                  pl.BlockSpec(bias.shape, lambda b, s, pt: (0, 0))]
        + _page_specs(_KT_BLOCK, l, pg, np_) + _page_specs(_KT_BLOCK, l, pg, np_),
        out_specs=pl.BlockSpec((ns * tq, BRANCH_W), lambda b, s, pt: (b, 0)),
        scratch_shapes=[pltpu.VMEM((ns, R, BRANCH_W), BF), pltpu.VMEM((ns, R, BRANCH_W), F32),
                        pltpu.VMEM((ns, BRANCH_W, LANES), F32),
                        pltpu.VMEM((ns, R, LANES), F32), pltpu.VMEM((ns, R, LANES), F32),
                        pltpu.VMEM((ns, nb, R, BRANCH_W), F32),
                        pltpu.VMEM((ns, PAGE, BRANCH_W), F32), pltpu.VMEM((ns, PAGE, BRANCH_W), F32)])
    n_pg = ns * pg
    return pl.pallas_call(
        functools.partial(_moba_dec_kernel, pg=pg, tq=tq, nb=nb), grid_spec=grid_spec,
        out_shape=jax.ShapeDtypeStruct((B * tq, BRANCH_W), F32),
        compiler_params=_params(("arbitrary", "arbitrary")), name="moba_decode",
    )(pt_flat, z, z, bias, *([pool_kt] * n_pg), *([pool_vt] * n_pg))


_HALO = 32


def _conv_kernel(z_ref, hist_ref, w_ref, b_ref, g_ref, beta_ref, y_ref, buf_ref, ext, *, tt):
    ti = pl.program_id(1)

    @pl.when(ti == 0)
    def _():
        ext[0:_HALO, :] = hist_ref[0]

    @pl.when(ti > 0)
    def _():
        ext[0:_HALO, :] = ext[tt:tt + _HALO, :]

    z = z_ref[...]
    ext[_HALO:_HALO + tt, :] = z[:, :BRANCH_W] * jax.nn.sigmoid(z[:, BRANCH_W:])
    first = _HALO - (CONV_W - 1)
    y = jnp.zeros((tt, BRANCH_W), F32)
    for j in range(CONV_W):
        y = y + w_ref[j:j + 1, :] * ext[first + j:first + j + tt, :]
    y = y + b_ref[...]
    mu = jnp.mean(y, axis=-1, keepdims=True)
    var = jnp.mean(jnp.square(y - mu), axis=-1, keepdims=True)
    yn = (y - mu) * lax.rsqrt(var + EPS) * g_ref[...] + beta_ref[...]
    y_ref[...] = yn * jax.nn.sigmoid(yn)

    @pl.when(ti == pl.num_programs(1) - 1)
    def _():
        buf_ref[0] = ext[tt + first:tt + _HALO, :]


def _conv(z, hist32, w_dw, b_dw, ln_g, ln_b, B, T, tt):
    nt = T // tt
    vec = lambda: pl.BlockSpec((1, BRANCH_W), lambda b, t: (0, 0))
    return pl.pallas_call(
        functools.partial(_conv_kernel, tt=tt), grid=(B, nt),
        in_specs=[pl.BlockSpec((None, tt, CHUNK), lambda b, t: (CH_CONV, b * nt + t, 0)),
                  pl.BlockSpec((1, _HALO, BRANCH_W), lambda b, t: (b, 0, 0)),
                  pl.BlockSpec((CONV_W, BRANCH_W), lambda b, t: (0, 0)), vec(), vec(), vec()],
        out_specs=[pl.BlockSpec((tt, BRANCH_W), lambda b, t: (b * nt + t, 0)),
                   pl.BlockSpec((1, CONV_W - 1, BRANCH_W), lambda b, t: (b, 0, 0))],
        out_shape=[jax.ShapeDtypeStruct((B * T, BRANCH_W), F32),
                   jax.ShapeDtypeStruct((B, CONV_W - 1, BRANCH_W), F32)],
        scratch_shapes=[pltpu.VMEM((tt + _HALO, BRANCH_W), F32)],
        compiler_params=_params(("arbitrary", "arbitrary")), name="conv_module",
    )(z, hist32, w_dw, b_dw.reshape(1, -1), ln_g.reshape(1, -1), ln_b.reshape(1, -1))


def _merge_kernel(x_ref, g1_ref, gate_ref, of_ref, om_ref, ob_ref, oc_ref, wb_ref, wo_ref, o_ref):
    merged = None
    for n, br in enumerate((of_ref, om_ref, ob_ref, oc_ref)):
        proj = jnp.dot(br[...].astype(BF), wb_ref[n], preferred_element_type=F32)
        sig = jnp.concatenate([gate_ref[2 * n], gate_ref[2 * n + 1]], axis=-1)
        merged = sig * proj if merged is None else merged + sig * proj
    out = jnp.dot(merged.astype(BF), wo_ref[...], preferred_element_type=F32)
    o_ref[...] = x_ref[...] + g1_ref[...] * out.reshape(x_ref.shape)


def _merge(tok, x, mod, l, z, o_fox, o_mla, o_moba, y_conv, w_branch, w_out):
    return pl.pallas_call(
        _merge_kernel, grid=(tok.n_tiles,),
        in_specs=[tok.x_spec(), tok.mod_spec(l, 2), tok.z_spec(0, 8)]
        + [tok.flat_spec(BRANCH_W)] * 4
        + [pl.BlockSpec(w_branch.shape, lambda i: (0, 0, 0)), pl.BlockSpec(w_out.shape, lambda i: (0, 0))],
        out_specs=tok.x_spec(),
        out_shape=jax.ShapeDtypeStruct(x.shape, F32),
        compiler_params=_params(("arbitrary",)), name="merge",
    )(x, mod, z, o_fox, o_mla, o_moba, y_conv, w_branch, w_out)


def _ffn_kernel(x_ref, sh_ref, sc_ref, g2_ref, gn_ref, wr_ref, br_ref, wg_ref, wu_ref, wd_ref, gout_ref,
                o_ref, h_scr, comb_scr, acc_e, acc_o, *, n_exp, final):
    e, f = pl.program_id(1), pl.program_id(2)
    nf = pl.num_programs(2)
    tm = h_scr.shape[0]

    @pl.when((e == 0) & (f == 0))
    def _():
        h = _modulated_norm(x_ref[...], gn_ref[...], sc_ref[...], sh_ref[...]).reshape(h_scr.shape)
        h_scr[...] = h.astype(BF)
        acc_o[...] = jnp.zeros(acc_o.shape, F32)
        if n_exp > 1:
            lane = lax.broadcasted_iota(jnp.int32, (tm, LANES), 1)
            logits = _dot_f32(h, wr_ref[...], _NT) + br_ref[...]
            logits = jnp.where(lane < n_exp, logits, NEG)
            m1 = jnp.max(logits, axis=-1, keepdims=True)
            i1 = jnp.min(jnp.where(logits == m1, lane, LANES), axis=-1, keepdims=True)
            rest = jnp.where(lane == i1, NEG, logits)
            m2 = jnp.max(rest, axis=-1, keepdims=True)
            i2 = jnp.min(jnp.where(rest == m2, lane, LANES), axis=-1, keepdims=True)
            e2 = jnp.exp(m2 - m1)
            w1 = 1.0 / (1.0 + e2)
            w2 = e2 / (1.0 + e2)
            comb_scr[...] = jnp.where(lane == i1, w1, 0.0) + jnp.where(lane == i2, w2, 0.0)

    @pl.when(f == 0)
    def _():
        acc_e[...] = jnp.zeros(acc_e.shape, F32)

    h = h_scr[...]
    a = jnp.dot(h, wg_ref[0], preferred_element_type=F32)
    u = jnp.dot(h, wu_ref[0], preferred_element_type=F32)
    act = (a * jax.nn.sigmoid(a) * u).astype(BF)
    acc_e[...] += jnp.dot(act, wd_ref[0], preferred_element_type=F32)

    @pl.when(f == nf - 1)
    def _():
        if n_exp > 1:
            lane = lax.broadcasted_iota(jnp.int32, (tm, LANES), 1)
            w = jnp.sum(jnp.where(lane == e, comb_scr[...], 0.0), axis=-1, keepdims=True)
            acc_o[...] += w * acc_e[...]
        else:
            acc_o[...] += acc_e[...]

    @pl.when((e == n_exp - 1) & (f == nf - 1))
    def _():
        xn = x_ref[...] + g2_ref[...] * acc_o[...].reshape(x_ref.shape)
        if final:
            ms = jnp.mean(xn * xn, axis=-1, keepdims=True)
            xn = xn * lax.rsqrt(ms + EPS) * gout_ref[...]
        o_ref[...] = xn


def _ffn(tok, x, mod, l, g_norm, w_router_t, b_router, wg, wu, wd, g_out, tf, final):
    n_exp, _, F = wg.shape
    nf = F // tf
    assert F % tf == 0 and tf % LANES == 0
    return pl.pallas_call(
        functools.partial(_ffn_kernel, n_exp=n_exp, final=final), grid=(tok.n_tiles, n_exp, nf),
        in_specs=[tok.x_spec(), tok.mod_spec(l, 3), tok.mod_spec(l, 4), tok.mod_spec(l, 5),
                  pl.BlockSpec((1, D_MODEL), lambda i, e, f: (0, 0)),
                  pl.BlockSpec(w_router_t.shape, lambda i, e, f: (0, 0)),
                  pl.BlockSpec((1, LANES), lambda i, e, f: (0, 0)),
                  pl.BlockSpec((1, D_MODEL, tf), lambda i, e, f: (e, 0, f)),
                  pl.BlockSpec((1, D_MODEL, tf), lambda i, e, f: (e, 0, f)),
                  pl.BlockSpec((1, tf, D_MODEL), lambda i, e, f: (e, f, 0)),
                  pl.BlockSpec((1, D_MODEL), lambda i, e, f: (0, 0))],
        out_specs=tok.x_spec(),
        out_shape=jax.ShapeDtypeStruct(x.shape, F32),
        scratch_shapes=[pltpu.VMEM((tok.tm, D_MODEL), BF), pltpu.VMEM((tok.tm, LANES), F32),
                        pltpu.VMEM((tok.tm, D_MODEL), F32), pltpu.VMEM((tok.tm, D_MODEL), F32)],
        compiler_params=_params(("arbitrary", "arbitrary", "arbitrary")), name="ffn",
    )(x, mod, mod, mod, g_norm.reshape(1, D_MODEL), w_router_t, b_router, wg, wu, wd, g_out.reshape(1, D_MODEL))


def _largest_tile(n, cap):
    t = min(n, cap)
    while n % t:
        t //= 2
    return t


def _ffn_tile(F):
    for tf in (512, 1408, 896, 256, 128):
        if F % tf == 0:
            return tf
    raise ValueError(F)


def kernel(x_prompt, x_sample, cache_fox_k, cache_fox_v, cache_fox_logf, cache_mla_ckv, cache_mla_krope,
           cache_moba_k, cache_moba_v, state_conv, page_table, c_prompt, c_sample,
           w_ada, b_ada, norm_mix, norm_ffn, w_in, b_fox_f, g_mla_q, g_mla_kv, w_mla_uq, w_mla_ukv,
           w_dw, b_dw, conv_ln_g, conv_ln_b, w_branch, w_out, rel_bias,
           ffn_w_gate, ffn_w_up, ffn_w_down, moe_w_router, moe_b_router, moe_w_gate, moe_w_up, moe_w_down,
           norm_out):
    Bp, Tp, D = x_prompt.shape
    Bs, Ts, _ = x_sample.shape
    L = w_ada.shape[0]
    np_ = page_table.shape[1]
    past_len = np_ * PAGE
    assert D == D_MODEL and Tp % MOBA_BLOCK == 0 and past_len % MOBA_BLOCK == 0 and Ts == 8
    assert Bs % SEQS_PER_STEP == 0
    pg = _largest_tile(np_, 16)

    n_c = Bp + Bs
    mc = -(-n_c // 8) * 8
    c_all = jnp.concatenate([c_prompt, c_sample, jnp.zeros((mc - n_c, D), F32)], axis=0)
    mod_all = _ada(c_all, w_ada, b_ada)
    mod_p = mod_all[:, :, :Bp, None, :]
    mod_s = mod_all[:, :, Bp:Bp + Bs, None, :]

    tok_p = _Tok(Bp, Tp, 1, _largest_tile(Tp, 512))
    tok_p_in = _Tok(Bp, Tp, 1, _largest_tile(Tp, 1024))
    tok_s = _Tok(Bs, Ts, _largest_tile(Bs, 64), Ts)
    tq_p = _largest_tile(Tp, 512)
    tk_p = _largest_tile(tq_p, 256)

    rope_p = _rope_tables(jnp.arange(Tp), 1)
    rope_s = _rope_tables(past_len + jnp.arange(Ts), tok_s.bb)

    tab = rel_bias.T.astype(F32)
    blk = MOBA_BLOCK
    tab_rows_p = jnp.tile(jnp.repeat(tab, blk, axis=0), (3, 1))
    kneg_p = (jnp.arange(3)[:, None, None] * blk - jnp.arange(blk)[None, None, :]
              + jnp.zeros((1, N_HEADS, 1), jnp.int32)).reshape(-1, 1).astype(jnp.int32)
    bias_p = _relbias(tab_rows_p, kneg_p, -jnp.arange(blk, dtype=jnp.int32)[None, :])
    bias_p = bias_p.reshape(3, N_HEADS, blk, blk)
    tab_rows_s = jnp.tile(jnp.repeat(tab, Ts, axis=0), (3, 1))
    qpos_s = (jnp.arange(3)[:, None, None] * blk + jnp.arange(Ts)[None, None, :]
              + jnp.zeros((1, N_HEADS, 1), jnp.int32)).reshape(-1, 1).astype(jnp.int32)
    bias_s = _relbias(tab_rows_s, qpos_s, jnp.arange(blk, dtype=jnp.int32)[None, :])

    pt_flat = page_table.reshape(-1).astype(jnp.int32)
    page_t = lambda a: jnp.transpose(a, (0, 1, 3, 4, 2))
    pk_fox, pv_fox = page_t(cache_fox_k), page_t(cache_fox_v)
    pk_moba, pv_moba = page_t(cache_moba_k), page_t(cache_moba_v)
    p_lft = jnp.swapaxes(cache_fox_logf, 2, 3)
    p_krt = jnp.swapaxes(cache_mla_krope, 2, 3)
    hist_p = jnp.zeros((Bp, _HALO, BRANCH_W), F32)
    hist_s = jnp.pad(state_conv, ((0, 0), (0, 0), (_HALO - (CONV_W - 1), 0), (0, 0)))

    xp, xs = x_prompt, x_sample
    rows_p, rows_s = [], []
    for l in range(L):
        wt_perm = _perm_w_in_t(w_in[l])
        w_uq = w_mla_uq[l]
        wn = w_uq[:, :, :MLA_NOPE].reshape(MLA_Q_RANK, -1).astype(BF)
        wr = w_uq[:, :, MLA_NOPE:].reshape(MLA_Q_RANK, -1).astype(BF)
        wuk = jnp.transpose(w_mla_ukv[l][:, :, :MLA_NOPE], (1, 2, 0)).astype(BF)
        wuv = jnp.transpose(w_mla_ukv[l][:, :, MLA_NOPE:], (1, 0, 2)).astype(BF)
        wuv_t = jnp.transpose(w_mla_ukv[l][:, :, MLA_NOPE:], (1, 2, 0)).astype(BF)
        bf128 = jnp.zeros((1, LANES), F32).at[0, MLA_ROPE:MLA_ROPE + N_HEADS].set(b_fox_f[l])
        gq, gkv = g_mla_q[l].reshape(1, -1), g_mla_kv[l].reshape(1, -1)
        wb, wo = w_branch[l].astype(BF), w_out[l].astype(BF)
        i = l // 2
        if l % 2 == 0:
            wg, wu, wd = ffn_w_gate[i][None].astype(BF), ffn_w_up[i][None].astype(BF), ffn_w_down[i][None].astype(BF)
            w_rt = jnp.zeros((LANES, D), F32)
            b_rt = jnp.zeros((1, LANES), F32)
        else:
            wg, wu, wd = moe_w_gate[i].astype(BF), moe_w_up[i].astype(BF), moe_w_down[i].astype(BF)
            n_e = wg.shape[0]
            w_rt = jnp.zeros((LANES, D), F32).at[:n_e].set(moe_w_router[i].T)
            b_rt = jnp.zeros((1, LANES), F32).at[0, :n_e].set(moe_b_router[i])
        tf = _ffn_tile(wg.shape[2])
        final = l == L - 1

        z = _inproj(tok_p_in, xp, mod_p, l, norm_mix[l], wt_perm)
        qlat, qrope, ckv, krlf = _prep(tok_p, z, gq, gkv, wn, wr, wuk, bf128, rope_p)
        logf = krlf[:, MLA_ROPE:MLA_ROPE + N_HEADS].reshape(Bp, Tp, N_HEADS)
        cum_t = _cumsum_time(jnp.swapaxes(logf, 1, 2).reshape(Bp * N_HEADS, Tp // LANES, LANES))
        cum_t = cum_t.reshape(Bp, N_HEADS, Tp)
        o_fox = _fox_prompt(z, jnp.swapaxes(cum_t, 1, 2), cum_t, Bp, Tp, tq_p, tk_p)
        o_mla = _mla_prompt(qlat, qrope, ckv, krlf, wuv_t, Bp, Tp, tq_p, tk_p)
        kmean = _kmean_prompt(z, Bp * Tp).reshape(Bp, Tp // blk, BRANCH_W)
        o_moba = _moba_prompt(z, kmean, bias_p, Bp, Tp)
        y_conv, buf_p = _conv(z, hist_p, w_dw[l], b_dw[l], conv_ln_g[l], conv_ln_b[l], Bp, Tp, tok_p.tt)
        xp = _merge(tok_p, xp, mod_p, l, z, o_fox, o_mla, o_moba, y_conv, wb, wo)
        xp = _ffn(tok_p_in, xp, mod_p, l, norm_ffn[l], w_rt, b_rt, wg, wu, wd, norm_out, tf, final)
        heads = lambda a: a.reshape(Bp, Tp, N_HEADS, HEAD_DIM)
        rows_p.append((heads(z[CH_FQK, :, BRANCH_W:]), heads(z[CH_FV_BQ, :, :BRANCH_W]), logf,
                       ckv.reshape(Bp, Tp, MLA_KV_RANK), krlf[:, :MLA_ROPE].reshape(Bp, Tp, MLA_ROPE),
                       heads(z[CH_BKV, :, :BRANCH_W]), heads(z[CH_BKV, :, BRANCH_W:]), buf_p))

        z = _inproj(tok_s, xs, mod_s, l, norm_mix[l], wt_perm)
        qlat, qrope, ckv, krlf = _prep(tok_s, z, gq, gkv, wn, wr, wuk, bf128, rope_s)
        logf = krlf[:, MLA_ROPE:MLA_ROPE + N_HEADS].reshape(Bs, Ts, N_HEADS)
        lf_new_t = jnp.pad(jnp.swapaxes(logf, 1, 2), ((0, 0), (0, 0), (0, LANES - Ts)))
        o_fox = _fox_decode(z, lf_new_t, pk_fox, pv_fox, p_lft, pt_flat, l, Bs, Ts, np_, pg)
        o_mla = _mla_decode(qlat, qrope, ckv, krlf, wuv, cache_mla_ckv, p_krt, pt_flat, l, Bs, Ts, np_, pg)
        o_moba = _moba_decode(z, bias_s, pk_moba, pv_moba, pt_flat, l, Bs, Ts, np_, pg)
        y_conv, buf_s = _conv(z, hist_s[l], w_dw[l], b_dw[l], conv_ln_g[l], conv_ln_b[l], Bs, Ts, Ts)
        xs = _merge(tok_s, xs, mod_s, l, z, o_fox, o_mla, o_moba, y_conv, wb, wo)
        xs = _ffn(tok_s, xs, mod_s, l, norm_ffn[l], w_rt, b_rt, wg, wu, wd, norm_out, tf, final)
        heads = lambda a: a.reshape(Bs, Ts, N_HEADS, HEAD_DIM)
        rows_s.append((heads(z[CH_FQK, :, BRANCH_W:]), heads(z[CH_FV_BQ, :, :BRANCH_W]), logf,
                       ckv.reshape(Bs, Ts, MLA_KV_RANK), krlf[:, :MLA_ROPE].reshape(Bs, Ts, MLA_ROPE),
                       heads(z[CH_BKV, :, :BRANCH_W]), heads(z[CH_BKV, :, BRANCH_W:]), buf_s))

    stack = lambda rows: [jnp.stack([r[j] for r in rows]) for j in range(8)]
    return (xp, xs, *stack(rows_p), *stack(rows_s))
```

```python
import functools
import math

import numpy as np
import jax
import jax.numpy as jnp
from jax import lax
from jax.experimental import pallas as pl
from jax.experimental.pallas import tpu as pltpu

F32 = jnp.float32
BF = jnp.bfloat16

D_MODEL = 1024
N_HEADS = 4
HEAD_DIM = 64
BRANCH_W = 256
MLA_Q_RANK = 256
MLA_KV_RANK = 128
MLA_NOPE = 64
MLA_ROPE = 32
CONV_W = 31
MOBA_BLOCK = 256
MOBA_TOPK = 3
REL_BUCKETS = 32
REL_MAX_DIST = 128
ROPE_THETA = 10000.0
EPS = 1e-6
PAGE = 128
IN_SIZES = (256, 256, 256, 4, 256, 128, 32, 256, 256, 256, 256, 256, 4096)

LANES = 128
CHUNK = 512
N_CHUNKS = 13
CH_FQK, CH_FV_BQ, CH_BKV, CH_MLA, CH_CONV = 8, 9, 10, 11, 12
NEG = -1e30
VMEM_LIMIT = 56 * 2 ** 20
QK_SCALE = HEAD_DIM ** -0.5


def _params(sem):
    return pltpu.CompilerParams(dimension_semantics=sem, vmem_limit_bytes=VMEM_LIMIT)


def _split3(x):
    x1 = x.astype(BF)
    r1 = x - x1.astype(F32)
    x2 = r1.astype(BF)
    x3 = (r1 - x2.astype(F32)).astype(BF)
    return x1, x2, x3


def _dot_exact01(x, m01):
    acc = None
    for piece in _split3(x):
        t = jnp.dot(piece, m01, preferred_element_type=F32)
        acc = t if acc is None else acc + t
    return acc


def _dot01_exact(m01, x):
    acc = None
    for piece in _split3(x):
        t = jnp.dot(m01, piece, preferred_element_type=F32)
        acc = t if acc is None else acc + t
    return acc


def _dot_f32(a, b, dn):
    a1, a2, a3 = _split3(a)
    b1, b2, b3 = _split3(b)
    acc = None
    for x, y in ((a1, b1), (a1, b2), (a2, b1), (a2, b2), (a1, b3), (a3, b1)):
        t = lax.dot_general(x, y, dn, preferred_element_type=F32)
        acc = t if acc is None else acc + t
    return acc


_NT = (((1,), (1,)), ((), ()))
_NN = (((1,), (0,)), ((), ()))


def _dot_nt(a, b):
    return lax.dot_general(a, b, _NT, preferred_element_type=F32)


def _log_sigmoid(x):
    return jnp.minimum(x, 0.0) - jnp.log1p(jnp.exp(-jnp.abs(x)))


def _rel_thresholds():
    max_exact = REL_BUCKETS // 2
    n = np.arange(0, 4 * REL_MAX_DIST)
    nf = np.maximum(n, 1).astype(np.float64)
    large = max_exact + (np.log(nf / max_exact) / math.log(REL_MAX_DIST / max_exact)
                         * (REL_BUCKETS - max_exact)).astype(np.int64)
    bucket = np.where(n < max_exact, n, np.minimum(large, REL_BUCKETS - 1))
    return [int(np.argmax(bucket >= k)) for k in range(1, REL_BUCKETS)]


_REL_THR = _rel_thresholds()


class _Tok:
    def __init__(self, B, T, bb, tt):
        assert B % bb == 0 and T % tt == 0 and (bb == 1 or tt == T) and tt % 8 == 0
        self.B, self.T, self.bb, self.tt = B, T, bb, tt
        self.nt = T // tt
        self.n_tiles = (B // bb) * self.nt
        self.tm = bb * tt
        self.N = B * T

    def x_spec(self):
        nt = self.nt
        return pl.BlockSpec((self.bb, self.tt, D_MODEL), lambda i, *_: (i // nt, i % nt, 0))

    def mod_spec(self, l, k):
        nt = self.nt
        return pl.BlockSpec((None, None, self.bb, 1, D_MODEL), lambda i, *_: (l, k, i // nt, 0, 0))

    def flat_spec(self, width):
        return pl.BlockSpec((self.tm, width), lambda i, *_: (i, 0))

    def z_spec(self, chunk, n=None):
        if n is None:
            return pl.BlockSpec((None, self.tm, CHUNK), lambda i, *_: (chunk, i, 0))
        return pl.BlockSpec((n, self.tm, CHUNK), lambda i, *_: (chunk // n, i, 0))


def _ada_kernel(c_ref, w_ref, b_ref, o_ref):
    c = c_ref[...]
    s = (c * jax.nn.sigmoid(c)).astype(BF)
    o_ref[...] = jnp.dot(s, w_ref[0].astype(BF), preferred_element_type=F32) + b_ref[0]


def _ada(c_all, w_ada, b_ada):
    L, D, D6 = w_ada.shape
    Mc = c_all.shape[0]
    tn = CHUNK
    per = D // tn
    return pl.pallas_call(
        _ada_kernel, grid=(L, D6 // tn),
        in_specs=[pl.BlockSpec((Mc, D), lambda l, j: (0, 0)),
                  pl.BlockSpec((1, D, tn), lambda l, j: (l, 0, j)),
                  pl.BlockSpec((1, 1, tn), lambda l, j: (l, 0, j))],
        out_specs=pl.BlockSpec((None, None, Mc, tn), lambda l, j: (l, j // per, 0, j % per)),
        out_shape=jax.ShapeDtypeStruct((L, D6 // D, Mc, D), F32),
        compiler_params=_params(("arbitrary", "arbitrary")), name="ada",
    )(c_all, w_ada, b_ada.reshape(L, 1, D6))


def _modulated_norm(x, g, sc, sh):
    ms = jnp.mean(x * x, axis=-1, keepdims=True)
    y = x * lax.rsqrt(ms + EPS) * g
    return y * (1.0 + sc) + sh


def _inproj_kernel(x_ref, sh_ref, sc_ref, g_ref, wt_ref, z_ref, h_scr):
    j = pl.program_id(1)

    @pl.when(j == 0)
    def _():
        h = _modulated_norm(x_ref[...], g_ref[...], sc_ref[...], sh_ref[...])
        h_scr[...] = h.reshape(h_scr.shape).astype(BF)

    acc = _dot_nt(h_scr[...], wt_ref[...])

    @pl.when(j < 8)
    def _():
        z_ref[...] = jax.nn.sigmoid(acc)

    @pl.when(j >= 8)
    def _():
        z_ref[...] = acc


def _inproj(tok, x, mod, l, g_norm, wt_perm):
    return pl.pallas_call(
        _inproj_kernel, grid=(tok.n_tiles, N_CHUNKS),
        in_specs=[tok.x_spec(), tok.mod_spec(l, 0), tok.mod_spec(l, 1),
                  pl.BlockSpec((1, D_MODEL), lambda i, j: (0, 0)),
                  pl.BlockSpec((CHUNK, D_MODEL), lambda i, j: (j, 0))],
        out_specs=pl.BlockSpec((None, tok.tm, CHUNK), lambda i, j: (j, i, 0)),
        out_shape=jax.ShapeDtypeStruct((N_CHUNKS, tok.N, CHUNK), F32),
        scratch_shapes=[pltpu.VMEM((tok.tm, D_MODEL), BF)],
        compiler_params=_params(("arbitrary", "arbitrary")), name="inproj",
    )(x, mod, mod, g_norm.reshape(1, D_MODEL), wt_perm)


def _perm_w_in_t(w):
    wt = w.T
    offs = np.concatenate([[0], np.cumsum(IN_SIZES)])
    fq, fk, fv, ff, mcq, mckv, mkr, bq, bk, bv, cu, cg, gate = [wt[offs[i]:offs[i + 1]] for i in range(13)]
    pad = jnp.zeros((LANES - MLA_ROPE - N_HEADS, w.shape[0]), w.dtype)
    return jnp.concatenate([gate, fq, fk, fv, bq, bk, bv, mcq, mckv, mkr, ff, pad, cu, cg], axis=0).astype(BF)


def _rope_lanes(x, c, sa, sb):
    return x * c + pltpu.roll(x, LANES - MLA_ROPE // 2, 1) * sa + pltpu.roll(x, MLA_ROPE // 2, 1) * sb


def _prep_kernel(z_ref, gq_ref, gkv_ref, wn_ref, wr_ref, wuk_ref, bf_ref, c_ref, sa_ref, sb_ref,
                 qlat_ref, qrope_ref, ckv_ref, krlf_ref):
    z = z_ref[...]
    mcq, mckv, kf = z[:, :256], z[:, 256:384], z[:, 384:512]
    qn = (mcq * lax.rsqrt(jnp.mean(mcq * mcq, axis=-1, keepdims=True) + EPS) * gq_ref[...]).astype(BF)
    q_nope = jnp.dot(qn, wn_ref[...], preferred_element_type=F32)
    q_rope = jnp.dot(qn, wr_ref[...], preferred_element_type=F32)
    c, sa, sb = c_ref[...], sa_ref[...], sb_ref[...]
    qrope_ref[...] = _rope_lanes(q_rope, c, sa, sb)
    for h in range(N_HEADS):
        qh = q_nope[:, h * MLA_NOPE:(h + 1) * MLA_NOPE].astype(BF)
        qlat_ref[:, h * MLA_KV_RANK:(h + 1) * MLA_KV_RANK] = jnp.dot(qh, wuk_ref[h], preferred_element_type=F32)
    ckv_ref[...] = mckv * lax.rsqrt(jnp.mean(mckv * mckv, axis=-1, keepdims=True) + EPS) * gkv_ref[...]
    lane = lax.broadcasted_iota(jnp.int32, kf.shape, 1)
    roped = _rope_lanes(kf, c, sa, sb)
    logf = _log_sigmoid(kf + bf_ref[...])
    krlf_ref[...] = jnp.where(lane < MLA_ROPE, roped, jnp.where(lane < MLA_ROPE + N_HEADS, logf, 0.0))


def _prep(tok, z, gq, gkv, wn, wr, wuk, bf128, rope_tabs):
    N = tok.N
    full = lambda a: pl.BlockSpec(a.shape, lambda i: (0,) * a.ndim)
    tab_spec = pl.BlockSpec((tok.tm, LANES), lambda i: (i % max(tok.nt, 1), 0)) if tok.bb == 1 else \
        pl.BlockSpec((tok.tm, LANES), lambda i: (0, 0))
    c, sa, sb = rope_tabs
    return pl.pallas_call(
        _prep_kernel, grid=(tok.n_tiles,),
        in_specs=[tok.z_spec(CH_MLA), full(gq), full(gkv), full(wn), full(wr), full(wuk), full(bf128),
                  tab_spec, tab_spec, tab_spec],
        out_specs=[tok.flat_spec(512), tok.flat_spec(128), tok.flat_spec(128), tok.flat_spec(128)],
        out_shape=[jax.ShapeDtypeStruct((N, 512), F32), jax.ShapeDtypeStruct((N, 128), F32),
                   jax.ShapeDtypeStruct((N, 128), F32), jax.ShapeDtypeStruct((N, 128), F32)],
        compiler_params=_params(("arbitrary",)), name="mla_prep",
    )(z, gq, gkv, wn, wr, wuk, bf128, c, sa, sb)


def _rope_tables(pos, reps):
    half = MLA_ROPE // 2
    inv = ROPE_THETA ** (-jnp.arange(half, dtype=F32) / half)
    ang = pos.astype(F32)[:, None] * inv[None, :]
    cos, sin = jnp.cos(ang), jnp.sin(ang)
    zero = jnp.zeros_like(sin)
    tile = lambda a, b: jnp.tile(jnp.concatenate([a, b], axis=1), (reps, LANES // MLA_ROPE))
    return tile(cos, cos), tile(-sin, zero), tile(zero, sin)


def _upper_tri():
    r = lax.broadcasted_iota(jnp.int32, (LANES, LANES), 0)
    c = lax.broadcasted_iota(jnp.int32, (LANES, LANES), 1)
    return (r <= c).astype(BF)


def _cumsum_kernel(x_ref, o_ref):
    x = x_ref[0]
    R = x.shape[0]
    within = _dot_exact01(x, _upper_tri())
    tot = jnp.broadcast_to(within[:, LANES - 1:LANES], (R, LANES))
    rr = lax.broadcasted_iota(jnp.int32, (R, R), 0)
    cc = lax.broadcasted_iota(jnp.int32, (R, R), 1)
    o_ref[0] = within + _dot01_exact((cc < rr).astype(BF), tot)


def _cumsum_time(x):
    G, R, _ = x.shape
    return pl.pallas_call(
        _cumsum_kernel, grid=(G,),
        in_specs=[pl.BlockSpec((1, R, LANES), lambda g: (g, 0, 0))],
        out_specs=pl.BlockSpec((1, R, LANES), lambda g: (g, 0, 0)),
        out_shape=jax.ShapeDtypeStruct(x.shape, F32),
        compiler_params=_params(("arbitrary",)), name="fox_cumsum",
    )(x)


def _init_stats(m_scr, l_scr, acc_scr):
    m_scr[...] = jnp.full(m_scr.shape, NEG, F32)
    l_scr[...] = jnp.zeros(l_scr.shape, F32)
    acc_scr[...] = jnp.zeros(acc_scr.shape, F32)


def _online_t(st, vt_bf, m_ref, l_ref, acc_ref, col_on=None, shift=None):
    m_prev = m_ref[...]
    m_tile = jnp.max(st, axis=0, keepdims=True)
    if shift is not None:
        m_tile = m_tile + shift
    m_new = jnp.maximum(m_prev, m_tile)
    if col_on is not None:
        m_new = jnp.where(col_on, m_new, m_prev)
    alpha = jnp.exp(m_prev - m_new)
    m_use = m_new if shift is None else m_new - shift
    if col_on is not None:
        m_use = jnp.where(col_on, m_use, -NEG)
    p = jnp.exp(st - m_use)
    l_ref[...] = alpha * l_ref[...] + jnp.sum(p, axis=0, keepdims=True)
    m_ref[...] = m_new
    acc_ref[...] = alpha * acc_ref[...] + jnp.dot(vt_bf, p.astype(BF), preferred_element_type=F32)


def _strip_plan(d, tk, tq, sw):
    for s in range(tq // sw):
        ss = slice(s * sw, (s + 1) * sw)
        if d is None:
            yield ss, None
            continue
        delta = s * sw - d * tk
        if delta <= -sw:
            continue
        yield ss, (None if delta >= tk - 1 else delta)


def _strip_width(tq):
    return tq


def _visible(tk, sw, delta):
    row = lax.broadcasted_iota(jnp.int32, (tk, sw), 0)
    col = lax.broadcasted_iota(jnp.int32, (tk, sw), 1)
    return row <= col + delta


def _attn_scratch(tq, dv):
    return [pltpu.VMEM((8, tq), F32), pltpu.VMEM((8, tq), F32), pltpu.VMEM((N_HEADS * dv, tq), F32)]


def _causal_branches(qi, ki, ratio, process):
    d = ki - ratio * qi

    @pl.when(d < 0)
    def _():
        process(None)

    for dd in range(ratio):
        @pl.when(d == dd)
        def _(dd=dd):
            process(dd)


def _fox_kernel(q_ref, k_ref, v_ref, ck_ref, cq_ref, o_ref, m_scr, l_scr, acc_scr, *, tq, tk, sw):
    qi, ki = pl.program_id(1), pl.program_id(2)

    @pl.when(ki == 0)
    def _():
        _init_stats(m_scr, l_scr, acc_scr)

    def process(d):
        q = (q_ref[:, :BRANCH_W] * QK_SCALE).astype(BF)
        k = k_ref[:, BRANCH_W:].astype(BF)
        vt = v_ref[:, :BRANCH_W].T.astype(BF)
        ck = ck_ref[0]
        cq = cq_ref[0]
        for h in range(N_HEADS):
            hs = slice(h * HEAD_DIM, (h + 1) * HEAD_DIM)
            for ss, delta in _strip_plan(d, tk, tq, sw):
                st = _dot_nt(k[:, hs], q[ss, hs]) + cq[h:h + 1, ss] - ck[:, h:h + 1]
                if delta is not None:
                    st = jnp.where(_visible(tk, sw, delta), st, NEG)
                _online_t(st, vt[hs, :], m_scr.at[h:h + 1, ss], l_scr.at[h:h + 1, ss], acc_scr.at[hs, ss])

    _causal_branches(qi, ki, tq // tk, process)

    @pl.when(ki == pl.num_programs(2) - 1)
    def _():
        ot = jnp.concatenate([acc_scr[h * HEAD_DIM:(h + 1) * HEAD_DIM, :] / l_scr[h:h + 1, :]
                              for h in range(N_HEADS)], axis=0)
        o_ref[...] = ot.T


def _fox_prompt(z, cum, cum_t, B, T, tq, tk):
    nq, nk, ratio = T // tq, T // tk, tq // tk
    last_k = lambda qi, ki: jnp.minimum(ki, ratio * qi + ratio - 1)
    zq = lambda ch: pl.BlockSpec((None, tq, CHUNK), lambda b, qi, ki: (ch, b * nq + qi, 0))
    zk = lambda ch: pl.BlockSpec((None, tk, CHUNK), lambda b, qi, ki: (ch, b * nk + last_k(qi, ki), 0))
    return pl.pallas_call(
        functools.partial(_fox_kernel, tq=tq, tk=tk, sw=_strip_width(tq)), grid=(B, nq, nk),
        in_specs=[zq(CH_FQK), zk(CH_FQK), zk(CH_FV_BQ),
                  pl.BlockSpec((1, tk, N_HEADS), lambda b, qi, ki: (b, last_k(qi, ki), 0)),
                  pl.BlockSpec((1, N_HEADS, tq), lambda b, qi, ki: (b, 0, qi))],
        out_specs=pl.BlockSpec((tq, BRANCH_W), lambda b, qi, ki: (b * nq + qi, 0)),
        out_shape=jax.ShapeDtypeStruct((B * T, BRANCH_W), F32),
        scratch_shapes=_attn_scratch(tq, HEAD_DIM),
        compiler_params=_params(("arbitrary", "arbitrary", "arbitrary")), name="fox_prompt",
    )(z, z, z, cum, cum_t)


_MLA_SCALE = (MLA_NOPE + MLA_ROPE) ** -0.5


def _mla_kernel(ql_ref, qr_ref, ckv_ref, kr_ref, wuvt_ref, o_ref, m_scr, l_scr, acc_scr, *, tq, tk, sw):
    qi, ki = pl.program_id(1), pl.program_id(2)

    @pl.when(ki == 0)
    def _():
        _init_stats(m_scr, l_scr, acc_scr)

    def process(d):
        ckv = ckv_ref[...]
        c_bf = ckv.astype(BF)
        ct_bf = ckv.T.astype(BF)
        kr = kr_ref[:, :MLA_ROPE].astype(BF)
        ql = ql_ref[...].astype(BF)
        qr = qr_ref[...].astype(BF)
        for h in range(N_HEADS):
            ls = slice(h * MLA_KV_RANK, (h + 1) * MLA_KV_RANK)
            rs = slice(h * MLA_ROPE, (h + 1) * MLA_ROPE)
            for ss, delta in _strip_plan(d, tk, tq, sw):
                st = (_dot_nt(c_bf, ql[ss, ls]) + _dot_nt(kr, qr[ss, rs])) * _MLA_SCALE
                if delta is not None:
                    st = jnp.where(_visible(tk, sw, delta), st, NEG)
                _online_t(st, ct_bf, m_scr.at[h:h + 1, ss], l_scr.at[h:h + 1, ss], acc_scr.at[ls, ss])

    _causal_branches(qi, ki, tq // tk, process)

    @pl.when(ki == pl.num_programs(2) - 1)
    def _():
        parts = []
        for h in range(N_HEADS):
            ls = slice(h * MLA_KV_RANK, (h + 1) * MLA_KV_RANK)
            o_lat_t = (acc_scr[ls, :] / l_scr[h:h + 1, :]).astype(BF)
            parts.append(jnp.dot(wuvt_ref[h], o_lat_t, preferred_element_type=F32))
        o_ref[...] = jnp.concatenate(parts, axis=0).T


def _mla_prompt(qlat, qrope, ckv, krlf, wuv_t, B, T, tq, tk):
    nq, nk, ratio = T // tq, T // tk, tq // tk
    qs = lambda w: pl.BlockSpec((tq, w), lambda b, qi, ki: (b * nq + qi, 0))
    ks = lambda w: pl.BlockSpec((tk, w), lambda b, qi, ki: (b * nk + jnp.minimum(ki, ratio * qi + ratio - 1), 0))
    return pl.pallas_call(
        functools.partial(_mla_kernel, tq=tq, tk=tk, sw=_strip_width(tq)), grid=(B, nq, nk),
        in_specs=[qs(512), qs(128), ks(128), ks(128),
                  pl.BlockSpec(wuv_t.shape, lambda b, qi, ki: (0, 0, 0))],
        out_specs=pl.BlockSpec((tq, BRANCH_W), lambda b, qi, ki: (b * nq + qi, 0)),
        out_shape=jax.ShapeDtypeStruct((B * T, BRANCH_W), F32),
        scratch_shapes=_attn_scratch(tq, MLA_KV_RANK),
        compiler_params=_params(("arbitrary", "arbitrary", "arbitrary")), name="mla_prompt",
    )(qlat, qrope, ckv, krlf, wuv_t)


def _relbias_kernel(tab_ref, qpos_ref, kpos_ref, o_ref):
    dist = jnp.maximum(qpos_ref[...] - kpos_ref[...], 0)
    tab = tab_ref[...]
    out = jnp.broadcast_to(tab[:, 0:1], dist.shape)
    for k in range(1, REL_BUCKETS):
        out = jnp.where(dist >= _REL_THR[k - 1], tab[:, k:k + 1], out)
    o_ref[...] = out


def _relbias(tab_rows, qpos, kpos):
    R, C = tab_rows.shape[0], kpos.shape[1]
    rt = min(R, 512)
    return pl.pallas_call(
        _relbias_kernel, grid=(R // rt,),
        in_specs=[pl.BlockSpec((rt, REL_BUCKETS), lambda i: (i, 0)),
                  pl.BlockSpec((rt, 1), lambda i: (i, 0)),
                  pl.BlockSpec((1, C), lambda i: (0, 0))],
        out_specs=pl.BlockSpec((rt, C), lambda i: (i, 0)),
        out_shape=jax.ShapeDtypeStruct((R, C), F32),
        compiler_params=_params(("arbitrary",)), name="rel_bias",
    )(tab_rows, qpos, kpos)


def _kmean_kernel(z_ref, o_ref):
    o_ref[0] = jnp.mean(z_ref[:, :BRANCH_W], axis=0, keepdims=True)


def _kmean_prompt(z, N):
    nb = N // MOBA_BLOCK
    return pl.pallas_call(
        _kmean_kernel, grid=(nb,),
        in_specs=[pl.BlockSpec((None, MOBA_BLOCK, CHUNK), lambda i: (CH_BKV, i, 0))],
        out_specs=pl.BlockSpec((1, 1, BRANCH_W), lambda i: (i, 0, 0)),
        out_shape=jax.ShapeDtypeStruct((nb, 1, BRANCH_W), F32),
        compiler_params=_params(("arbitrary",)), name="moba_kmean",
    )(z)


def _top_blocks(gate, n_valid, n_blocks):
    lane = lax.broadcasted_iota(jnp.int32, gate.shape, 1)
    rank = jnp.zeros(gate.shape, F32)
    for m in range(n_blocks):
        gm = gate[:, m:m + 1]
        ahead = jnp.where(gm > gate, 1.0, jnp.where(gm == gate, jnp.where(lane > m, 1.0, 0.0), 0.0))
        rank = rank + ahead * jnp.where(m < n_valid, 1.0, 0.0)
    return jnp.where(lane < n_valid, jnp.where(rank < MOBA_TOPK, 1.0, 0.0), 0.0)


def _top_blocks_t(gate_t, n_valid, n_blocks):
    row = lax.broadcasted_iota(jnp.int32, gate_t.shape, 0)
    rank = jnp.zeros(gate_t.shape, F32)
    for m in range(n_blocks):
        gm = gate_t[m:m + 1, :]
        ahead = jnp.where(gm > gate_t, 1.0, jnp.where(gm == gate_t, jnp.where(row > m, 1.0, 0.0), 0.0))
        rank = rank + ahead * jnp.where(m < n_valid, 1.0, 0.0)
    return jnp.where(row < n_valid, jnp.where(rank < MOBA_TOPK, 1.0, 0.0), 0.0)


def _moba_kernel(q_ref, kv_ref, km_ref, bias_ref, o_ref, m_scr, l_scr, acc_scr, sel_scr, *, nb, sw):
    qi, ki = pl.program_id(1), pl.program_id(2)
    tq = tk = MOBA_BLOCK

    @pl.when(ki == 0)
    def _():
        _init_stats(m_scr, l_scr, acc_scr)
        q = q_ref[:, BRANCH_W:]
        km = km_ref[0]
        for h in range(N_HEADS):
            hs = slice(h * HEAD_DIM, (h + 1) * HEAD_DIM)
            g_t = _dot_f32(km[:, hs], q[:, hs], _NT)
            sel_t = _top_blocks_t(g_t, qi, nb)
            for n in range(nb):
                sel_scr[h, n] = sel_t[n:n + 1, :]

    def process(kind):
        q = (q_ref[:, BRANCH_W:] * QK_SCALE).astype(BF)
        k = kv_ref[:, :BRANCH_W].astype(BF)
        vt = kv_ref[:, BRANCH_W:].T.astype(BF)
        for h in range(N_HEADS):
            hs = slice(h * HEAD_DIM, (h + 1) * HEAD_DIM)
            for s in range(tq // sw):
                ss = slice(s * sw, (s + 1) * sw)
                st = _dot_nt(k[:, hs], q[ss, hs])
                stats = (m_scr.at[h:h + 1, ss], l_scr.at[h:h + 1, ss], acc_scr.at[hs, ss])
                if kind == "diag":
                    st = jnp.where(_visible(tk, sw, s * sw), st + bias_ref[0, h, :, ss], NEG)
                    _online_t(st, vt[hs, :], *stats)
                    continue
                picked = sel_scr[h, ki][:, ss] > 0.0
                if kind == "near":
                    _online_t(st + bias_ref[0, h, :, ss], vt[hs, :], *stats, col_on=picked)
                else:
                    _online_t(st, vt[hs, :], *stats, col_on=picked, shift=bias_ref[0, h, 0:1, 0:1])

    @pl.when(ki < qi - 1)
    def _():
        process("far")

    @pl.when(ki == qi - 1)
    def _():
        process("near")

    @pl.when(ki == qi)
    def _():
        process("diag")

    @pl.when(ki == pl.num_programs(2) - 1)
    def _():
        ot = jnp.concatenate([acc_scr[h * HEAD_DIM:(h + 1) * HEAD_DIM, :] / l_scr[h:h + 1, :]
                              for h in range(N_HEADS)], axis=0)
        o_ref[...] = ot.T


def _moba_prompt(z, kmean, bias, B, T):
    tq = MOBA_BLOCK
    nq = T // tq
    assert nq <= LANES
    return pl.pallas_call(
        functools.partial(_moba_kernel, nb=nq, sw=_strip_width(tq)), grid=(B, nq, nq),
        in_specs=[pl.BlockSpec((None, tq, CHUNK), lambda b, qi, ki: (CH_FV_BQ, b * nq + qi, 0)),
                  pl.BlockSpec((None, tq, CHUNK), lambda b, qi, ki: (CH_BKV, b * nq + jnp.minimum(ki, qi), 0)),
                  pl.BlockSpec((1, nq, BRANCH_W), lambda b, qi, ki: (b, 0, 0)),
                  pl.BlockSpec((1, N_HEADS, tq, tq),
                               lambda b, qi, ki: (jnp.clip(qi - ki, 0, 2), 0, 0, 0))],
        out_specs=pl.BlockSpec((tq, BRANCH_W), lambda b, qi, ki: (b * nq + qi, 0)),
        out_shape=jax.ShapeDtypeStruct((B * T, BRANCH_W), F32),
        scratch_shapes=_attn_scratch(tq, HEAD_DIM) + [pltpu.VMEM((N_HEADS, nq, 1, tq), F32)],
        compiler_params=_params(("arbitrary", "arbitrary", "arbitrary")), name="moba_prompt",
    )(z, z, kmean, bias)


def _head_rows(x, width):
    return jnp.concatenate([x[:, h * width:(h + 1) * width] for h in range(N_HEADS)], axis=0)


def _blockdiag_rows(q):
    lane_head = lax.broadcasted_iota(jnp.int32, q.shape, 1) // HEAD_DIM
    return jnp.concatenate([jnp.where(lane_head == h, q, 0.0) for h in range(N_HEADS)], axis=0)


def _diag_heads(o_all, tq):
    lane_head = lax.broadcasted_iota(jnp.int32, (tq, o_all.shape[1]), 1) // HEAD_DIM
    out = jnp.zeros((tq, o_all.shape[1]), F32)
    for h in range(N_HEADS):
        out = out + jnp.where(lane_head == h, o_all[h * tq:(h + 1) * tq], 0.0)
    return out


def _rows_per_head(x, tq):
    return jnp.concatenate([jnp.broadcast_to(x[h:h + 1], (tq, x.shape[1])) for h in range(N_HEADS)], axis=0)


SEQS_PER_STEP = 2


def _page_specs(block, l, pg, np_):
    zeros = (0,) * (len(block) - 2)
    ns = SEQS_PER_STEP
    return [pl.BlockSpec(block, lambda b, s, pt, j=j, p=p: (l, pt[(ns * b + j) * np_ + s * pg + p]) + zeros)
            for j in range(ns) for p in range(pg)]


_KT_BLOCK = (None, None, N_HEADS, HEAD_DIM, PAGE)


def _kt(ref):
    return ref[...].reshape(BRANCH_W, PAGE).astype(BF)


def _new_rows_mask(R, tq):
    row_t = lax.broadcasted_iota(jnp.int32, (R, LANES), 0) % tq
    col = lax.broadcasted_iota(jnp.int32, (R, LANES), 1)
    return col <= row_t


def _tile_update(s_all, pv_fn, m_scr, l_scr, acc_scr):
    m_prev = m_scr[...][:, :1]
    m_new = jnp.maximum(m_prev, jnp.max(s_all, axis=-1, keepdims=True))
    alpha = jnp.exp(m_prev - m_new)
    p = jnp.exp(s_all - m_new)
    l_new = alpha * l_scr[...][:, :1] + jnp.sum(p, axis=-1, keepdims=True)
    l_scr[...] = jnp.broadcast_to(l_new, l_scr.shape)
    m_scr[...] = jnp.broadcast_to(m_new, m_scr.shape)
    acc_scr[...] = alpha * acc_scr[...] + pv_fn(p.astype(BF))


def _fox_dec_kernel(pt_ref, zq_ref, zv_ref, lfn_ref, *rest, pg, tq):
    ns = SEQS_PER_STEP
    n_pg = ns * pg
    k_refs, v_refs, lf_refs = rest[:n_pg], rest[n_pg:2 * n_pg], rest[2 * n_pg:3 * n_pg]
    o_ref = rest[3 * n_pg]
    qbd, m_scr, l_scr, acc_scr, run_scr, padk, padv = rest[3 * n_pg + 1:]
    s_idx = pl.program_id(1)
    n_steps = pl.num_programs(1)
    R = N_HEADS * tq
    G = pg * N_HEADS

    @pl.when(s_idx == 0)
    def _():
        for j in range(ns):
            _init_stats(m_scr.at[j], l_scr.at[j], acc_scr.at[j])
            qbd[j] = _blockdiag_rows(zq_ref[j * tq:(j + 1) * tq, :BRANCH_W] * QK_SCALE).astype(BF)
            run_scr[j] = jnp.zeros((G, LANES), F32)

    def tile(j, with_new):
        kj, vj, lfj = (r[j * pg:(j + 1) * pg] for r in (k_refs, v_refs, lf_refs))
        upper = _upper_tri()
        lf = jnp.concatenate([lfj[p][...] for p in range(pg)], axis=0)
        within = _dot_exact01(lf, upper)
        tot = jnp.broadcast_to(within[:, LANES - 1:LANES], (G, LANES))
        r = lax.broadcasted_iota(jnp.int32, (G, G), 0)
        c = lax.broadcasted_iota(jnp.int32, (G, G), 1)
        same_head = (r % N_HEADS) == (c % N_HEADS)
        earlier = jnp.where(same_head, jnp.where(c // N_HEADS < r // N_HEADS, 1.0, 0.0), 0.0).astype(BF)
        run = run_scr[j]
        fk = run + _dot01_exact(earlier, tot) + within
        run_new = run + _dot01_exact(jnp.where(same_head, 1.0, 0.0).astype(BF), tot)
        run_scr[j] = run_new
        q = qbd[j]
        parts = [jnp.dot(q, _kt(kj[p]), preferred_element_type=F32)
                 - _rows_per_head(fk[p * N_HEADS:(p + 1) * N_HEADS], tq) for p in range(pg)]
        if with_new:
            lfn8 = jnp.concatenate([lfn_ref[j], jnp.zeros((8 - N_HEADS, LANES), F32)], axis=0)
            fkn = run_new[0:8] + _dot_exact01(lfn8, upper)
            s_new = _dot_nt(q, padk[j].astype(BF)) - _rows_per_head(fkn, tq)
            parts.append(jnp.where(_new_rows_mask(R, tq), s_new, NEG))

        def pv_fn(p):
            out = None
            for i in range(pg):
                t = _dot_nt(p[:, i * PAGE:(i + 1) * PAGE], _kt(vj[i]))
                out = t if out is None else out + t
            if with_new:
                out = out + jnp.dot(p[:, pg * PAGE:], padv[j].astype(BF), preferred_element_type=F32)
            return out

        _tile_update(jnp.concatenate(parts, axis=1), pv_fn, m_scr.at[j], l_scr.at[j], acc_scr.at[j])

    @pl.when(s_idx < n_steps - 1)
    def _():
        for j in range(ns):
            tile(j, False)

    @pl.when(s_idx == n_steps - 1)
    def _():
        padk[...] = jnp.zeros(padk.shape, F32)
        padv[...] = jnp.zeros(padv.shape, F32)
        for j in range(ns):
            padk[j, 0:tq, :] = zq_ref[j * tq:(j + 1) * tq, BRANCH_W:]
            padv[j, 0:tq, :] = zv_ref[j * tq:(j + 1) * tq, :BRANCH_W]
        for j in range(ns):
            tile(j, True)
        for j in range(ns):
            o_ref[j * tq:(j + 1) * tq, :] = _diag_heads(acc_scr[j] / l_scr[j][:, :1], tq)


def _fox_decode(z, lf_new_t, pool_kt, pool_vt, pool_lft, pt_flat, l, B, tq, np_, pg):
    R = N_HEADS * tq
    ns = SEQS_PER_STEP
    grid_spec = pltpu.PrefetchScalarGridSpec(
        num_scalar_prefetch=1, grid=(B // ns, np_ // pg),
        in_specs=[pl.BlockSpec((None, ns * tq, CHUNK), lambda b, s, pt: (CH_FQK, b, 0)),
                  pl.BlockSpec((None, ns * tq, CHUNK), lambda b, s, pt: (CH_FV_BQ, b, 0)),
                  pl.BlockSpec((ns, N_HEADS, LANES), lambda b, s, pt: (b, 0, 0))]
        + _page_specs(_KT_BLOCK, l, pg, np_) + _page_specs(_KT_BLOCK, l, pg, np_)
        + _page_specs((None, None, N_HEADS, PAGE), l, pg, np_),
        out_specs=pl.BlockSpec((ns * tq, BRANCH_W), lambda b, s, pt: (b, 0)),
        scratch_shapes=[pltpu.VMEM((ns, R, BRANCH_W), BF), pltpu.VMEM((ns, R, LANES), F32),
                        pltpu.VMEM((ns, R, LANES), F32), pltpu.VMEM((ns, R, BRANCH_W), F32),
                        pltpu.VMEM((ns, pg * N_HEADS, LANES), F32),
                        pltpu.VMEM((ns, PAGE, BRANCH_W), F32), pltpu.VMEM((ns, PAGE, BRANCH_W), F32)])
    n_pg = ns * pg
    return pl.pallas_call(
        functools.partial(_fox_dec_kernel, pg=pg, tq=tq), grid_spec=grid_spec,
        out_shape=jax.ShapeDtypeStruct((B * tq, BRANCH_W), F32),
        compiler_params=_params(("arbitrary", "arbitrary")), name="fox_decode",
    )(pt_flat, z, z, lf_new_t, *([pool_kt] * n_pg), *([pool_vt] * n_pg), *([pool_lft] * n_pg))


def _mla_dec_kernel(pt_ref, ql_ref, qr_ref, cn_ref, kn_ref, wuv_ref, *rest, pg, tq):
    ns = SEQS_PER_STEP
    n_pg = ns * pg
    c_refs, r_refs = rest[:n_pg], rest[n_pg:2 * n_pg]
    o_ref = rest[2 * n_pg]
    ql_scr, qr_scr, m_scr, l_scr, acc_scr, padc, padr = rest[2 * n_pg + 1:]
    s_idx = pl.program_id(1)
    n_steps = pl.num_programs(1)
    R = N_HEADS * tq

    @pl.when(s_idx == 0)
    def _():
        for j in range(ns):
            _init_stats(m_scr.at[j], l_scr.at[j], acc_scr.at[j])
            ql_scr[j] = _head_rows(ql_ref[j * tq:(j + 1) * tq, :], MLA_KV_RANK).astype(BF)
            qr_scr[j] = _head_rows(qr_ref[j * tq:(j + 1) * tq, :], MLA_ROPE).astype(BF)

    def tile(j, with_new):
        cj, rj = c_refs[j * pg:(j + 1) * pg], r_refs[j * pg:(j + 1) * pg]
        ql, qr = ql_scr[j], qr_scr[j]
        cs = [cj[p][...].astype(BF) for p in range(pg)]
        parts = [(_dot_nt(ql, cs[p]) + jnp.dot(qr, rj[p][...].astype(BF), preferred_element_type=F32))
                 * _MLA_SCALE for p in range(pg)]
        if with_new:
            s_new = (_dot_nt(ql, padc[j].astype(BF)) + _dot_nt(qr, padr[j].astype(BF))) * _MLA_SCALE
            parts.append(jnp.where(_new_rows_mask(R, tq), s_new, NEG))

        def pv_fn(p):
            out = None
            for i in range(pg):
                t = jnp.dot(p[:, i * PAGE:(i + 1) * PAGE], cs[i], preferred_element_type=F32)
                out = t if out is None else out + t
            if with_new:
                out = out + jnp.dot(p[:, pg * PAGE:], padc[j].astype(BF), preferred_element_type=F32)
            return out

        _tile_update(jnp.concatenate(parts, axis=1), pv_fn, m_scr.at[j], l_scr.at[j], acc_scr.at[j])

    @pl.when(s_idx < n_steps - 1)
    def _():
        for j in range(ns):
            tile(j, False)

    @pl.when(s_idx == n_steps - 1)
    def _():
        padc[...] = jnp.zeros(padc.shape, F32)
        padr[...] = jnp.zeros(padr.shape, F32)
        for j in range(ns):
            padc[j, 0:tq, :] = cn_ref[j * tq:(j + 1) * tq, :]
            padr[j, 0:tq, :] = kn_ref[j * tq:(j + 1) * tq, :MLA_ROPE]
        for j in range(ns):
            tile(j, True)
        for j in range(ns):
            o_lat = (acc_scr[j] / l_scr[j][:, :1]).astype(BF)
            for h in range(N_HEADS):
                o_ref[j * tq:(j + 1) * tq, h * HEAD_DIM:(h + 1) * HEAD_DIM] = jnp.dot(
                    o_lat[h * tq:(h + 1) * tq], wuv_ref[h], preferred_element_type=F32)


def _mla_decode(qlat, qrope, ckv, krlf, wuv, pool_c, pool_rt, pt_flat, l, B, tq, np_, pg):
    R = N_HEADS * tq
    ns = SEQS_PER_STEP
    row = lambda w: pl.BlockSpec((ns * tq, w), lambda b, s, pt: (b, 0))
    grid_spec = pltpu.PrefetchScalarGridSpec(
        num_scalar_prefetch=1, grid=(B // ns, np_ // pg),
        in_specs=[row(512), row(128), row(128), row(128),
                  pl.BlockSpec(wuv.shape, lambda b, s, pt: (0, 0, 0))]
        + _page_specs((None, None, PAGE, MLA_KV_RANK), l, pg, np_)
        + _page_specs((None, None, MLA_ROPE, PAGE), l, pg, np_),
        out_specs=pl.BlockSpec((ns * tq, BRANCH_W), lambda b, s, pt: (b, 0)),
        scratch_shapes=[pltpu.VMEM((ns, R, MLA_KV_RANK), BF), pltpu.VMEM((ns, R, MLA_ROPE), BF),
                        pltpu.VMEM((ns, R, LANES), F32), pltpu.VMEM((ns, R, LANES), F32),
                        pltpu.VMEM((ns, R, MLA_KV_RANK), F32),
                        pltpu.VMEM((ns, PAGE, MLA_KV_RANK), F32), pltpu.VMEM((ns, PAGE, MLA_ROPE), F32)])
    n_pg = ns * pg
    return pl.pallas_call(
        functools.partial(_mla_dec_kernel, pg=pg, tq=tq), grid_spec=grid_spec,
        out_shape=jax.ShapeDtypeStruct((B * tq, BRANCH_W), F32),
        compiler_params=_params(("arbitrary", "arbitrary")), name="mla_decode",
    )(pt_flat, qlat, qrope, ckv, krlf, wuv, *([pool_c] * n_pg), *([pool_rt] * n_pg))


def _moba_dec_kernel(pt_ref, zq_ref, zkv_ref, bias_ref, *rest, pg, tq, nb):
    ns = SEQS_PER_STEP
    n_pg = ns * pg
    k_refs, v_refs = rest[:n_pg], rest[n_pg:2 * n_pg]
    o_ref = rest[2 * n_pg]
    qbd, qf_scr, km_scr, mb_scr, lb_scr, ob_scr, padk, padv = rest[2 * n_pg + 1:]
    s_idx = pl.program_id(1)
    R = N_HEADS * tq

    @pl.when(s_idx == 0)
    def _():
        for j in range(ns):
            q = _blockdiag_rows(zq_ref[j * tq:(j + 1) * tq, BRANCH_W:] * QK_SCALE)
            qf_scr[j] = q
            qbd[j] = q.astype(BF)
        km_scr[...] = jnp.zeros(km_scr.shape, F32)
        mb_scr[...] = jnp.zeros(mb_scr.shape, F32)
        lb_scr[...] = jnp.zeros(lb_scr.shape, F32)

    def block_stats(s_parts, pv_fn):
        m = s_parts[0].max(axis=-1, keepdims=True)
        for s in s_parts[1:]:
            m = jnp.maximum(m, s.max(axis=-1, keepdims=True))
        ps = [jnp.exp(s - m) for s in s_parts]
        lsum = ps[0].sum(axis=-1, keepdims=True)
        for p in ps[1:]:
            lsum = lsum + p.sum(axis=-1, keepdims=True)
        return m, lsum, pv_fn([p.astype(BF) for p in ps])

    b_far = bias_ref[2 * R:3 * R, :]
    b_last = bias_ref[R:2 * R, :]
    lane = lax.broadcasted_iota(jnp.int32, (BRANCH_W, LANES), 1)
    lane_r = lax.broadcasted_iota(jnp.int32, (R, LANES), 1)
    for i in range(pg // 2):
        n = s_idx * (pg // 2) + i
        bias = jnp.where(n == nb - 1, b_last, b_far)
        for j in range(ns):
            ka, kb = k_refs[j * pg + 2 * i], k_refs[j * pg + 2 * i + 1]
            va, vb = v_refs[j * pg + 2 * i], v_refs[j * pg + 2 * i + 1]
            kta, ktb = ka[...].reshape(BRANCH_W, PAGE), kb[...].reshape(BRANCH_W, PAGE)
            kmean = jnp.sum(kta + ktb, axis=-1, keepdims=True) * (1.0 / MOBA_BLOCK)
            km_scr[j] = jnp.where(lane == n, kmean, km_scr[j])
            q = qbd[j]
            s_a = jnp.dot(q, kta.astype(BF), preferred_element_type=F32) + bias[:, :PAGE]
            s_b = jnp.dot(q, ktb.astype(BF), preferred_element_type=F32) + bias[:, PAGE:]
            m, lsum, o = block_stats(
                [s_a, s_b], lambda ps, va=va, vb=vb: _dot_nt(ps[0], _kt(va)) + _dot_nt(ps[1], _kt(vb)))
            mb_scr[j] = jnp.where(lane_r == n, m, mb_scr[j])
            lb_scr[j] = jnp.where(lane_r == n, lsum, lb_scr[j])
            ob_scr[j, n] = o

    @pl.when(s_idx == pl.num_programs(1) - 1)
    def _():
        padk[...] = jnp.zeros(padk.shape, F32)
        padv[...] = jnp.zeros(padv.shape, F32)
        for j in range(ns):
            padk[j, 0:tq, :] = zkv_ref[j * tq:(j + 1) * tq, :BRANCH_W]
            padv[j, 0:tq, :] = zkv_ref[j * tq:(j + 1) * tq, BRANCH_W:]
        for j in range(ns):
            s_own = jnp.where(_new_rows_mask(R, tq),
                              _dot_nt(qbd[j], padk[j].astype(BF)) + bias_ref[0:R, :PAGE], NEG)
            m_own, l_own, o_own = block_stats(
                [s_own], lambda ps, j=j: jnp.dot(ps[0], padv[j].astype(BF), preferred_element_type=F32))
            g = _dot_f32(qf_scr[j], km_scr[j], _NN)
            on = _top_blocks(g, nb, nb) > 0.0
            m_blk = mb_scr[j]
            m_tot = jnp.maximum(m_own, jnp.max(jnp.where(on, m_blk, NEG), axis=-1, keepdims=True))
            w = jnp.where(on, jnp.exp(m_blk - m_tot), 0.0)
            a_own = jnp.exp(m_own - m_tot)
            l_tot = a_own * l_own + jnp.sum(w * lb_scr[j], axis=-1, keepdims=True)
            o_tot = a_own * o_own
            for n in range(nb):
                o_tot = o_tot + w[:, n:n + 1] * ob_scr[j, n]
            o_ref[j * tq:(j + 1) * tq, :] = _diag_heads(o_tot / l_tot, tq)


def _moba_decode(z, bias, pool_kt, pool_vt, pt_flat, l, B, tq, np_, pg):
    R = N_HEADS * tq
    nb = np_ * PAGE // MOBA_BLOCK
    assert pg % 2 == 0 and nb <= LANES
    ns = SEQS_PER_STEP
    grid_spec = pltpu.PrefetchScalarGridSpec(
        num_scalar_prefetch=1, grid=(B // ns, np_ // pg),
        in_specs=[pl.BlockSpec((None, ns * tq, CHUNK), lambda b, s, pt: (CH_FV_BQ, b, 0)),
                  pl.BlockSpec((None, ns * tq, CHUNK), lambda b, s, pt: (CH_BKV, b, 0)),
                  pl.BlockSpec(bias.shape, lambda b, s, pt: (0, 0))]
        + _page_specs(_KT_BLOCK, l, pg, np_) + _page_specs(_KT_BLOCK, l, pg, np_),
        out_specs=pl.BlockSpec((ns * tq, BRANCH_W), lambda b, s, pt: (b, 0)),
        scratch_shapes=[pltpu.VMEM((ns, R, BRANCH_W), BF), pltpu.VMEM((ns, R, BRANCH_W), F32),
                        pltpu.VMEM((ns, BRANCH_W, LANES), F32),
                        pltpu.VMEM((ns, R, LANES), F32), pltpu.VMEM((ns, R, LANES), F32),
                        pltpu.VMEM((ns, nb, R, BRANCH_W), F32),
                        pltpu.VMEM((ns, PAGE, BRANCH_W), F32), pltpu.VMEM((ns, PAGE, BRANCH_W), F32)])
    n_pg = ns * pg
    return pl.pallas_call(
        functools.partial(_moba_dec_kernel, pg=pg, tq=tq, nb=nb), grid_spec=grid_spec,
        out_shape=jax.ShapeDtypeStruct((B * tq, BRANCH_W), F32),
        compiler_params=_params(("arbitrary", "arbitrary")), name="moba_decode",
    )(pt_flat, z, z, bias, *([pool_kt] * n_pg), *([pool_vt] * n_pg))


_HALO = 32


def _conv_kernel(z_ref, hist_ref, w_ref, b_ref, g_ref, beta_ref, y_ref, buf_ref, ext, *, tt):
    ti = pl.program_id(1)

    @pl.when(ti == 0)
    def _():
        ext[0:_HALO, :] = hist_ref[0]

    @pl.when(ti > 0)
    def _():
        ext[0:_HALO, :] = ext[tt:tt + _HALO, :]

    z = z_ref[...]
    ext[_HALO:_HALO + tt, :] = z[:, :BRANCH_W] * jax.nn.sigmoid(z[:, BRANCH_W:])
    first = _HALO - (CONV_W - 1)
    y = jnp.zeros((tt, BRANCH_W), F32)
    for j in range(CONV_W):
        y = y + w_ref[j:j + 1, :] * ext[first + j:first + j + tt, :]
    y = y + b_ref[...]
    mu = jnp.mean(y, axis=-1, keepdims=True)
    var = jnp.mean(jnp.square(y - mu), axis=-1, keepdims=True)
    yn = (y - mu) * lax.rsqrt(var + EPS) * g_ref[...] + beta_ref[...]
    y_ref[...] = yn * jax.nn.sigmoid(yn)

    @pl.when(ti == pl.num_programs(1) - 1)
    def _():
        buf_ref[0] = ext[tt + first:tt + _HALO, :]


def _conv(z, hist32, w_dw, b_dw, ln_g, ln_b, B, T, tt):
    nt = T // tt
    vec = lambda: pl.BlockSpec((1, BRANCH_W), lambda b, t: (0, 0))
    return pl.pallas_call(
        functools.partial(_conv_kernel, tt=tt), grid=(B, nt),
        in_specs=[pl.BlockSpec((None, tt, CHUNK), lambda b, t: (CH_CONV, b * nt + t, 0)),
                  pl.BlockSpec((1, _HALO, BRANCH_W), lambda b, t: (b, 0, 0)),
                  pl.BlockSpec((CONV_W, BRANCH_W), lambda b, t: (0, 0)), vec(), vec(), vec()],
        out_specs=[pl.BlockSpec((tt, BRANCH_W), lambda b, t: (b * nt + t, 0)),
                   pl.BlockSpec((1, CONV_W - 1, BRANCH_W), lambda b, t: (b, 0, 0))],
        out_shape=[jax.ShapeDtypeStruct((B * T, BRANCH_W), F32),
                   jax.ShapeDtypeStruct((B, CONV_W - 1, BRANCH_W), F32)],
        scratch_shapes=[pltpu.VMEM((tt + _HALO, BRANCH_W), F32)],
        compiler_params=_params(("arbitrary", "arbitrary")), name="conv_module",
    )(z, hist32, w_dw, b_dw.reshape(1, -1), ln_g.reshape(1, -1), ln_b.reshape(1, -1))


def _merge_kernel(x_ref, g1_ref, gate_ref, of_ref, om_ref, ob_ref, oc_ref, wb_ref, wo_ref, o_ref):
    merged = None
    for n, br in enumerate((of_ref, om_ref, ob_ref, oc_ref)):
        proj = jnp.dot(br[...].astype(BF), wb_ref[n], preferred_element_type=F32)
        sig = jnp.concatenate([gate_ref[2 * n], gate_ref[2 * n + 1]], axis=-1)
        merged = sig * proj if merged is None else merged + sig * proj
    out = jnp.dot(merged.astype(BF), wo_ref[...], preferred_element_type=F32)
    o_ref[...] = x_ref[...] + g1_ref[...] * out.reshape(x_ref.shape)


def _merge(tok, x, mod, l, z, o_fox, o_mla, o_moba, y_conv, w_branch, w_out):
    return pl.pallas_call(
        _merge_kernel, grid=(tok.n_tiles,),
        in_specs=[tok.x_spec(), tok.mod_spec(l, 2), tok.z_spec(0, 8)]
        + [tok.flat_spec(BRANCH_W)] * 4
        + [pl.BlockSpec(w_branch.shape, lambda i: (0, 0, 0)), pl.BlockSpec(w_out.shape, lambda i: (0, 0))],
        out_specs=tok.x_spec(),
        out_shape=jax.ShapeDtypeStruct(x.shape, F32),
        compiler_params=_params(("arbitrary",)), name="merge",
    )(x, mod, z, o_fox, o_mla, o_moba, y_conv, w_branch, w_out)


def _ffn_kernel(x_ref, sh_ref, sc_ref, g2_ref, gn_ref, wr_ref, br_ref, wg_ref, wu_ref, wd_ref, gout_ref,
                o_ref, h_scr, comb_scr, acc_e, acc_o, *, n_exp, final):
    e, f = pl.program_id(1), pl.program_id(2)
    nf = pl.num_programs(2)
    tm = h_scr.shape[0]

    @pl.when((e == 0) & (f == 0))
    def _():
        h = _modulated_norm(x_ref[...], gn_ref[...], sc_ref[...], sh_ref[...]).reshape(h_scr.shape)
        h_scr[...] = h.astype(BF)
        acc_o[...] = jnp.zeros(acc_o.shape, F32)
        if n_exp > 1:
            lane = lax.broadcasted_iota(jnp.int32, (tm, LANES), 1)
            logits = _dot_f32(h, wr_ref[...], _NT) + br_ref[...]
            logits = jnp.where(lane < n_exp, logits, NEG)
            m1 = jnp.max(logits, axis=-1, keepdims=True)
            i1 = jnp.min(jnp.where(logits == m1, lane, LANES), axis=-1, keepdims=True)
            rest = jnp.where(lane == i1, NEG, logits)
            m2 = jnp.max(rest, axis=-1, keepdims=True)
            i2 = jnp.min(jnp.where(rest == m2, lane, LANES), axis=-1, keepdims=True)
            e2 = jnp.exp(m2 - m1)
            w1 = 1.0 / (1.0 + e2)
            w2 = e2 / (1.0 + e2)
            comb_scr[...] = jnp.where(lane == i1, w1, 0.0) + jnp.where(lane == i2, w2, 0.0)

    @pl.when(f == 0)
    def _():
        acc_e[...] = jnp.zeros(acc_e.shape, F32)

    h = h_scr[...]
    a = jnp.dot(h, wg_ref[0], preferred_element_type=F32)
    u = jnp.dot(h, wu_ref[0], preferred_element_type=F32)
    act = (a * jax.nn.sigmoid(a) * u).astype(BF)
    acc_e[...] += jnp.dot(act, wd_ref[0], preferred_element_type=F32)

    @pl.when(f == nf - 1)
    def _():
        if n_exp > 1:
            lane = lax.broadcasted_iota(jnp.int32, (tm, LANES), 1)
            w = jnp.sum(jnp.where(lane == e, comb_scr[...], 0.0), axis=-1, keepdims=True)
            acc_o[...] += w * acc_e[...]
        else:
            acc_o[...] += acc_e[...]

    @pl.when((e == n_exp - 1) & (f == nf - 1))
    def _():
        xn = x_ref[...] + g2_ref[...] * acc_o[...].reshape(x_ref.shape)
        if final:
            ms = jnp.mean(xn * xn, axis=-1, keepdims=True)
            xn = xn * lax.rsqrt(ms + EPS) * gout_ref[...]
        o_ref[...] = xn


def _ffn(tok, x, mod, l, g_norm, w_router_t, b_router, wg, wu, wd, g_out, tf, final):
    n_exp, _, F = wg.shape
    nf = F // tf
    assert F % tf == 0 and tf % LANES == 0
    return pl.pallas_call(
        functools.partial(_ffn_kernel, n_exp=n_exp, final=final), grid=(tok.n_tiles, n_exp, nf),
        in_specs=[tok.x_spec(), tok.mod_spec(l, 3), tok.mod_spec(l, 4), tok.mod_spec(l, 5),
                  pl.BlockSpec((1, D_MODEL), lambda i, e, f: (0, 0)),
                  pl.BlockSpec(w_router_t.shape, lambda i, e, f: (0, 0)),
                  pl.BlockSpec((1, LANES), lambda i, e, f: (0, 0)),
                  pl.BlockSpec((1, D_MODEL, tf), lambda i, e, f: (e, 0, f)),
                  pl.BlockSpec((1, D_MODEL, tf), lambda i, e, f: (e, 0, f)),
                  pl.BlockSpec((1, tf, D_MODEL), lambda i, e, f: (e, f, 0)),
                  pl.BlockSpec((1, D_MODEL), lambda i, e, f: (0, 0))],
        out_specs=tok.x_spec(),
        out_shape=jax.ShapeDtypeStruct(x.shape, F32),
        scratch_shapes=[pltpu.VMEM((tok.tm, D_MODEL), BF), pltpu.VMEM((tok.tm, LANES), F32),
                        pltpu.VMEM((tok.tm, D_MODEL), F32), pltpu.VMEM((tok.tm, D_MODEL), F32)],
        compiler_params=_params(("arbitrary", "arbitrary", "arbitrary")), name="ffn",
    )(x, mod, mod, mod, g_norm.reshape(1, D_MODEL), w_router_t, b_router, wg, wu, wd, g_out.reshape(1, D_MODEL))


def _largest_tile(n, cap):
    t = min(n, cap)
    while n % t:
        t //= 2
    return t


def _ffn_tile(F, wide=False):
    for tf in ((1792, 1408) if wide else ()) + (512, 1408, 896, 256, 128):
        if F % tf == 0:
            return tf
    raise ValueError(F)


def kernel(x_prompt, x_sample, cache_fox_k, cache_fox_v, cache_fox_logf, cache_mla_ckv, cache_mla_krope,
           cache_moba_k, cache_moba_v, state_conv, page_table, c_prompt, c_sample,
           w_ada, b_ada, norm_mix, norm_ffn, w_in, b_fox_f, g_mla_q, g_mla_kv, w_mla_uq, w_mla_ukv,
           w_dw, b_dw, conv_ln_g, conv_ln_b, w_branch, w_out, rel_bias,
           ffn_w_gate, ffn_w_up, ffn_w_down, moe_w_router, moe_b_router, moe_w_gate, moe_w_up, moe_w_down,
           norm_out):
    Bp, Tp, D = x_prompt.shape
    Bs, Ts, _ = x_sample.shape
    L = w_ada.shape[0]
    np_ = page_table.shape[1]
    past_len = np_ * PAGE
    assert D == D_MODEL and Tp % MOBA_BLOCK == 0 and past_len % MOBA_BLOCK == 0 and Ts == 8
    assert Bs % SEQS_PER_STEP == 0
    pg = _largest_tile(np_, 16)

    n_c = Bp + Bs
    mc = -(-n_c // 8) * 8
    c_all = jnp.concatenate([c_prompt, c_sample, jnp.zeros((mc - n_c, D), F32)], axis=0)
    mod_all = _ada(c_all, w_ada, b_ada)
    mod_p = mod_all[:, :, :Bp, None, :]
    mod_s = mod_all[:, :, Bp:Bp + Bs, None, :]

    tok_p = _Tok(Bp, Tp, 1, _largest_tile(Tp, 512))
    tok_p_in = _Tok(Bp, Tp, 1, _largest_tile(Tp, 1024))
    tok_s = _Tok(Bs, Ts, _largest_tile(Bs, 64), Ts)
    tq_p = _largest_tile(Tp, 512)
    tk_p = _largest_tile(tq_p, 512)

    rope_p = _rope_tables(jnp.arange(Tp), 1)
    rope_s = _rope_tables(past_len + jnp.arange(Ts), tok_s.bb)

    tab = rel_bias.T.astype(F32)
    blk = MOBA_BLOCK
    tab_rows_p = jnp.tile(jnp.repeat(tab, blk, axis=0), (3, 1))
    kneg_p = (jnp.arange(3)[:, None, None] * blk - jnp.arange(blk)[None, None, :]
              + jnp.zeros((1, N_HEADS, 1), jnp.int32)).reshape(-1, 1).astype(jnp.int32)
    bias_p = _relbias(tab_rows_p, kneg_p, -jnp.arange(blk, dtype=jnp.int32)[None, :])
    bias_p = bias_p.reshape(3, N_HEADS, blk, blk)
    tab_rows_s = jnp.tile(jnp.repeat(tab, Ts, axis=0), (3, 1))
    qpos_s = (jnp.arange(3)[:, None, None] * blk + jnp.arange(Ts)[None, None, :]
              + jnp.zeros((1, N_HEADS, 1), jnp.int32)).reshape(-1, 1).astype(jnp.int32)
    bias_s = _relbias(tab_rows_s, qpos_s, jnp.arange(blk, dtype=jnp.int32)[None, :])

    pt_flat = page_table.reshape(-1).astype(jnp.int32)
    page_t = lambda a: jnp.transpose(a, (0, 1, 3, 4, 2))
    pk_fox, pv_fox = page_t(cache_fox_k), page_t(cache_fox_v)
    pk_moba, pv_moba = page_t(cache_moba_k), page_t(cache_moba_v)
    p_lft = jnp.swapaxes(cache_fox_logf, 2, 3)
    p_krt = jnp.swapaxes(cache_mla_krope, 2, 3)
    hist_p = jnp.zeros((Bp, _HALO, BRANCH_W), F32)
    hist_s = jnp.pad(state_conv, ((0, 0), (0, 0), (_HALO - (CONV_W - 1), 0), (0, 0)))

    xp, xs = x_prompt, x_sample
    rows_p, rows_s = [], []
    for l in range(L):
        wt_perm = _perm_w_in_t(w_in[l])
        w_uq = w_mla_uq[l]
        wn = w_uq[:, :, :MLA_NOPE].reshape(MLA_Q_RANK, -1).astype(BF)
        wr = w_uq[:, :, MLA_NOPE:].reshape(MLA_Q_RANK, -1).astype(BF)
        wuk = jnp.transpose(w_mla_ukv[l][:, :, :MLA_NOPE], (1, 2, 0)).astype(BF)
        wuv = jnp.transpose(w_mla_ukv[l][:, :, MLA_NOPE:], (1, 0, 2)).astype(BF)
        wuv_t = jnp.transpose(w_mla_ukv[l][:, :, MLA_NOPE:], (1, 2, 0)).astype(BF)
        bf128 = jnp.zeros((1, LANES), F32).at[0, MLA_ROPE:MLA_ROPE + N_HEADS].set(b_fox_f[l])
        gq, gkv = g_mla_q[l].reshape(1, -1), g_mla_kv[l].reshape(1, -1)
        wb, wo = w_branch[l].astype(BF), w_out[l].astype(BF)
        i = l // 2
        if l % 2 == 0:
            wg, wu, wd = ffn_w_gate[i][None].astype(BF), ffn_w_up[i][None].astype(BF), ffn_w_down[i][None].astype(BF)
            w_rt = jnp.zeros((LANES, D), F32)
            b_rt = jnp.zeros((1, LANES), F32)
        else:
            wg, wu, wd = moe_w_gate[i].astype(BF), moe_w_up[i].astype(BF), moe_w_down[i].astype(BF)
            n_e = wg.shape[0]
            w_rt = jnp.zeros((LANES, D), F32).at[:n_e].set(moe_w_router[i].T)
            b_rt = jnp.zeros((1, LANES), F32).at[0, :n_e].set(moe_b_router[i])
        tf = _ffn_tile(wg.shape[2])
        final = l == L - 1

        z = _inproj(tok_p_in, xp, mod_p, l, norm_mix[l], wt_perm)
        qlat, qrope, ckv, krlf = _prep(tok_p, z, gq, gkv, wn, wr, wuk, bf128, rope_p)
        logf = krlf[:, MLA_ROPE:MLA_ROPE + N_HEADS].reshape(Bp, Tp, N_HEADS)
        cum_t = _cumsum_time(jnp.swapaxes(logf, 1, 2).reshape(Bp * N_HEADS, Tp // LANES, LANES))
        cum_t = cum_t.reshape(Bp, N_HEADS, Tp)
        o_fox = _fox_prompt(z, jnp.swapaxes(cum_t, 1, 2), cum_t, Bp, Tp, tq_p, tk_p)
        o_mla = _mla_prompt(qlat, qrope, ckv, krlf, wuv_t, Bp, Tp, tq_p, tk_p)
        kmean = _kmean_prompt(z, Bp * Tp).reshape(Bp, Tp // blk, BRANCH_W)
        o_moba = _moba_prompt(z, kmean, bias_p, Bp, Tp)
        y_conv, buf_p = _conv(z, hist_p, w_dw[l], b_dw[l], conv_ln_g[l], conv_ln_b[l], Bp, Tp, tok_p.tt)
        xp = _merge(tok_p, xp, mod_p, l, z, o_fox, o_mla, o_moba, y_conv, wb, wo)
        xp = _ffn(tok_p_in, xp, mod_p, l, norm_ffn[l], w_rt, b_rt, wg, wu, wd, norm_out, tf, final)
        heads = lambda a: a.reshape(Bp, Tp, N_HEADS, HEAD_DIM)
        rows_p.append((heads(z[CH_FQK, :, BRANCH_W:]), heads(z[CH_FV_BQ, :, :BRANCH_W]), logf,
                       ckv.reshape(Bp, Tp, MLA_KV_RANK), krlf[:, :MLA_ROPE].reshape(Bp, Tp, MLA_ROPE),
                       heads(z[CH_BKV, :, :BRANCH_W]), heads(z[CH_BKV, :, BRANCH_W:]), buf_p))

        z = _inproj(tok_s, xs, mod_s, l, norm_mix[l], wt_perm)
        qlat, qrope, ckv, krlf = _prep(tok_s, z, gq, gkv, wn, wr, wuk, bf128, rope_s)
        logf = krlf[:, MLA_ROPE:MLA_ROPE + N_HEADS].reshape(Bs, Ts, N_HEADS)
        lf_new_t = jnp.pad(jnp.swapaxes(logf, 1, 2), ((0, 0), (0, 0), (0, LANES - Ts)))
        o_fox = _fox_decode(z, lf_new_t, pk_fox, pv_fox, p_lft, pt_flat, l, Bs, Ts, np_, pg)
        o_mla = _mla_decode(qlat, qrope, ckv, krlf, wuv, cache_mla_ckv, p_krt, pt_flat, l, Bs, Ts, np_, pg)
        o_moba = _moba_decode(z, bias_s, pk_moba, pv_moba, pt_flat, l, Bs, Ts, np_, pg)
        y_conv, buf_s = _conv(z, hist_s[l], w_dw[l], b_dw[l], conv_ln_g[l], conv_ln_b[l], Bs, Ts, Ts)
        xs = _merge(tok_s, xs, mod_s, l, z, o_fox, o_mla, o_moba, y_conv, wb, wo)
        xs = _ffn(tok_s, xs, mod_s, l, norm_ffn[l], w_rt, b_rt, wg, wu, wd, norm_out,
                  _ffn_tile(wg.shape[2], wide=True), final)
        heads = lambda a: a.reshape(Bs, Ts, N_HEADS, HEAD_DIM)
        rows_s.append((heads(z[CH_FQK, :, BRANCH_W:]), heads(z[CH_FV_BQ, :, :BRANCH_W]), logf,
                       ckv.reshape(Bs, Ts, MLA_KV_RANK), krlf[:, :MLA_ROPE].reshape(Bs, Ts, MLA_ROPE),
                       heads(z[CH_BKV, :, :BRANCH_W]), heads(z[CH_BKV, :, BRANCH_W:]), buf_s))

    stack = lambda rows: [jnp.stack([r[j] for r in rows]) for j in range(8)]
    return (xp, xs, *stack(rows_p), *stack(rows_s))
```
